```python
import jax, jax.numpy as jnp
from jax import lax
import numpy as np

D_MODEL = 1024
BATCH = 8
SEQ = 4096
DEPTH = 2

GRID_W = 64
HEAD_DIM = 64
FNET_GROUPS = 4
FNET_WIDTH = FNET_GROUPS * HEAD_DIM
SGU_HEADS = 4
SGU_WIDTH = SGU_HEADS * HEAD_DIM
SGU_CHUNK = 128
N_Q_HEADS = 8
N_KV_HEADS = 2
Q_PER_KV = N_Q_HEADS // N_KV_HEADS
Q_WIDTH = N_Q_HEADS * HEAD_DIM
KV_WIDTH = N_KV_HEADS * HEAD_DIM
Q_BLOCK = 128
ROPE_THETA = 10000.0
ROPE_FREQS = HEAD_DIM // 4
D_MIX = FNET_WIDTH + SGU_WIDTH + Q_WIDTH
D_IN = FNET_WIDTH + 2 * SGU_WIDTH + Q_WIDTH + 2 * KV_WIDTH
D_FF = -(-8 * D_MODEL // (3 * 256)) * 256
EPS = 1e-6

kernel_name = "hybrid_fnet_sgu_gqa_encoder"


def rmsnorm(x, g):
    xf = x.astype(jnp.float32)
    y = xf * lax.rsqrt(jnp.mean(xf * xf, axis=-1, keepdims=True) + EPS)
    return (y * g.astype(jnp.float32)).astype(x.dtype)


def layernorm(x, g):
    xf = x.astype(jnp.float32)
    mu = jnp.mean(xf, axis=-1, keepdims=True)
    var = jnp.mean(jnp.square(xf - mu), axis=-1, keepdims=True)
    y = (xf - mu) * lax.rsqrt(var + EPS)
    return (y * g.astype(jnp.float32)).astype(x.dtype)


def fourier_mix(h):
    b, s, _ = h.shape
    hg = h.reshape(b, s, FNET_GROUPS, HEAD_DIM).astype(jnp.float32)
    y = jnp.fft.fft2(hg, axes=(1, 3), norm="ortho").real
    return y.reshape(b, s, FNET_WIDTH).astype(h.dtype)


def spatial_gating(z, w_s, b_s, g_v):
    b, s, _ = z.shape
    z = jax.nn.gelu(z)
    u, v = z[..., :SGU_WIDTH], z[..., SGU_WIDTH:]
    v = layernorm(v, g_v)
    v = v.reshape(b, s // SGU_CHUNK, SGU_CHUNK, SGU_HEADS, HEAD_DIM)
    sv = jnp.einsum('hpq,bnqhc->bnphc', w_s.astype(v.dtype), v) + b_s.T[:, :, None].astype(v.dtype)
    return u * sv.reshape(b, s, SGU_WIDTH)


def axial_rope_tables(s, dtype):
    rows = s // GRID_W
    row = jnp.repeat(jnp.arange(rows), GRID_W).astype(jnp.float32)
    col = jnp.tile(jnp.arange(GRID_W), rows).astype(jnp.float32)
    freqs = ROPE_THETA ** (-jnp.arange(ROPE_FREQS, dtype=jnp.float32) / ROPE_FREQS)
    ang_r = row[:, None] * freqs
    ang_c = col[:, None] * freqs
    return (jnp.cos(ang_r).astype(dtype), jnp.sin(ang_r).astype(dtype),
            jnp.cos(ang_c).astype(dtype), jnp.sin(ang_c).astype(dtype))


def rope_half(x, cos, sin):
    x1, x2 = x[..., :ROPE_FREQS], x[..., ROPE_FREQS:]
    c, sn = cos[:, None, :], sin[:, None, :]
    return jnp.concatenate([x1 * c - x2 * sn, x2 * c + x1 * sn], axis=-1)


def apply_axial_rope(x, tabs):
    cos_r, sin_r, cos_c, sin_c = tabs
    half = HEAD_DIM // 2
    return jnp.concatenate([rope_half(x[..., :half], cos_r, sin_r),
                            rope_half(x[..., half:], cos_c, sin_c)], axis=-1)


def gqa_attention(q, k, v, g_q, g_k, tabs):
    b, s, _ = q.shape
    q = rmsnorm(q.reshape(b, s, N_Q_HEADS, HEAD_DIM), g_q)
    k = rmsnorm(k.reshape(b, s, N_KV_HEADS, HEAD_DIM), g_k)
    v = v.reshape(b, s, N_KV_HEADS, HEAD_DIM)
    q = apply_axial_rope(q, tabs)
    k = apply_axial_rope(k, tabs)
    qb = q.reshape(b, s // Q_BLOCK, Q_BLOCK, N_KV_HEADS, Q_PER_KV, HEAD_DIM).transpose(1, 0, 2, 3, 4, 5)
    scale = HEAD_DIM ** -0.5

    def block(qblk):
        sc = jnp.einsum('bqkgd,bskd->bkgqs', qblk, k).astype(jnp.float32) * scale
        p = jax.nn.softmax(sc, axis=-1).astype(v.dtype)
        return jnp.einsum('bkgqs,bskd->bqkgd', p, v)

    o = lax.map(block, qb)
    return o.transpose(1, 0, 2, 3, 4, 5).reshape(b, s, Q_WIDTH)


def hybrid_mixer(h, w_in, sgu_w, sgu_b, sgu_g, g_q, g_k, g_mix, w_out, tabs):
    z = h @ w_in
    o0 = FNET_WIDTH
    o1 = o0 + 2 * SGU_WIDTH
    o2 = o1 + Q_WIDTH
    o3 = o2 + KV_WIDTH
    y_f = fourier_mix(z[..., :o0])
    y_s = spatial_gating(z[..., o0:o1], sgu_w, sgu_b, sgu_g)
    y_a = gqa_attention(z[..., o1:o2], z[..., o2:o3], z[..., o3:], g_q, g_k, tabs)
    a0 = FNET_WIDTH
    a1 = a0 + SGU_WIDTH
    y = jnp.concatenate([rmsnorm(y_f, g_mix[:a0]),
                         rmsnorm(y_s, g_mix[a0:a1]),
                         rmsnorm(y_a, g_mix[a1:])], axis=-1)
    return y @ w_out


def swiglu(h, w_gate, w_up, w_down):
    return (jax.nn.silu(h @ w_gate) * (h @ w_up)) @ w_down


def setup_inputs(seed: int = 0) -> dict:
    key = jax.random.key(seed)
    ks = jax.random.split(key, 17)
    f32 = jnp.float32

    def gain(k, shape):
        return (1.0 + 0.02 * jax.random.normal(k, shape, f32)).astype(f32)

    def dense(k, shape, fan_in):
        return (jax.random.normal(k, shape, f32) * fan_in ** -0.5).astype(f32)

    return {
        "x": jax.random.normal(ks[0], (BATCH, SEQ, D_MODEL), f32),
        "g_pre_mix": gain(ks[1], (DEPTH, D_MODEL)),
        "w_in": dense(ks[2], (DEPTH, D_MODEL, D_IN), D_MODEL),
        "sgu_w": dense(ks[3], (DEPTH, SGU_HEADS, SGU_CHUNK, SGU_CHUNK), SGU_CHUNK),
        "sgu_b": gain(ks[4], (DEPTH, SGU_HEADS, SGU_CHUNK)),
        "sgu_g": gain(ks[5], (DEPTH, SGU_WIDTH)),
        "g_q": gain(ks[6], (DEPTH, HEAD_DIM)),
        "g_k": gain(ks[7], (DEPTH, HEAD_DIM)),
        "g_mix": gain(ks[8], (DEPTH, D_MIX)),
        "w_out": dense(ks[9], (DEPTH, D_MIX, D_MODEL), D_MIX),
        "g_post_mix": gain(ks[10], (DEPTH, D_MODEL)),
        "g_pre_ffn": gain(ks[11], (DEPTH, D_MODEL)),
        "w_gate": dense(ks[12], (DEPTH, D_MODEL, D_FF), D_MODEL),
        "w_up": dense(ks[13], (DEPTH, D_MODEL, D_FF), D_MODEL),
        "w_down": dense(ks[14], (DEPTH, D_FF, D_MODEL), D_FF),
        "g_post_ffn": gain(ks[15], (DEPTH, D_MODEL)),
    }


def reference(x, g_pre_mix, w_in, sgu_w, sgu_b, sgu_g, g_q, g_k, g_mix, w_out,
              g_post_mix, g_pre_ffn, w_gate, w_up, w_down, g_post_ffn):
    tabs = axial_rope_tables(x.shape[1], x.dtype)
    for l in range(DEPTH):
        h = rmsnorm(x, g_pre_mix[l])
        m = hybrid_mixer(h, w_in[l], sgu_w[l], sgu_b[l], sgu_g[l], g_q[l], g_k[l],
                         g_mix[l], w_out[l], tabs)
        x = x + rmsnorm(m, g_post_mix[l])
        h = rmsnorm(x, g_pre_ffn[l])
        f = swiglu(h, w_gate[l], w_up[l], w_down[l])
        x = x + rmsnorm(f, g_post_ffn[l])
    return x
```

```python
import functools
import math

import numpy as np
import jax
import jax.numpy as jnp
from jax import lax
from jax.experimental import pallas as pl
from jax.experimental.pallas import tpu as pltpu

F32 = jnp.float32
BF16 = jnp.bfloat16

D_MODEL = 1024
GRID_W = 64
HEAD_DIM = 64
FNET_WIDTH = 256
SGU_WIDTH = 256
SGU_HEADS = 4
SGU_CHUNK = 128
N_Q_HEADS = 8
N_KV_HEADS = 2
Q_PER_KV = N_Q_HEADS // N_KV_HEADS
Q_WIDTH = N_Q_HEADS * HEAD_DIM
KV_WIDTH = N_KV_HEADS * HEAD_DIM
ROPE_THETA = 10000.0
ROPE_FREQS = HEAD_DIM // 4
D_MIX = FNET_WIDTH + SGU_WIDTH + Q_WIDTH
D_IN = FNET_WIDTH + 2 * SGU_WIDTH + Q_WIDTH + 2 * KV_WIDTH
EPS = 1e-6

LANES = 128
VMEM_LIMIT = 56 * 1024 * 1024

O_SGU = FNET_WIDTH
O_Q = O_SGU + 2 * SGU_WIDTH
O_K = O_Q + Q_WIDTH
O_V = O_K + KV_WIDTH


def _params(*sem):
    return pltpu.CompilerParams(dimension_semantics=sem, vmem_limit_bytes=VMEM_LIMIT)


def _rms(x, g):
    return x * lax.rsqrt(jnp.mean(x * x, axis=-1, keepdims=True) + EPS) * g


def _head_norm_rope(xs, g, cos, sin, seg_ones, swap_lo):
    ms = jnp.dot((xs * xs).astype(BF16), seg_ones, preferred_element_type=F32) * (1.0 / HEAD_DIM)
    xn = xs * lax.rsqrt(ms + EPS) * g
    sw = jnp.where(swap_lo, pltpu.roll(xn, LANES - ROPE_FREQS, 1), pltpu.roll(xn, ROPE_FREQS, 1))
    return xn * cos + sw * sin


def _inproj_kernel(x_ref, g_ref, w_ref, gq_ref, gk_ref, cos_ref, sin_ref,
                   zf_ref, zs_ref, q_ref, k_ref, v_ref):
    h = _rms(x_ref[...], g_ref[...]).astype(BF16)
    z = jnp.dot(h, w_ref[...], preferred_element_type=F32)
    for j in range(FNET_WIDTH // LANES):
        zf_ref[j] = z[:, j * LANES:(j + 1) * LANES]
    zs_ref[...] = z[:, O_SGU:O_Q]
    r = lax.broadcasted_iota(jnp.int32, (LANES, LANES), 0) // HEAD_DIM
    c = lax.broadcasted_iota(jnp.int32, (LANES, LANES), 1) // HEAD_DIM
    seg_ones = jnp.where(r == c, 1.0, 0.0).astype(BF16)
    lane = lax.broadcasted_iota(jnp.int32, (1, LANES), 1)
    swap_lo = (lane % (2 * ROPE_FREQS)) < ROPE_FREQS
    cos = cos_ref[...]
    sin = sin_ref[...]
    scale = HEAD_DIM ** -0.5
    for j in range(Q_WIDTH // LANES):
        qs = z[:, O_Q + j * LANES:O_Q + (j + 1) * LANES]
        qr = _head_norm_rope(qs, gq_ref[...], cos, sin, seg_ones, swap_lo)
        q_ref[:, j * LANES:(j + 1) * LANES] = (qr * scale).astype(BF16)
    k_ref[...] = _head_norm_rope(z[:, O_K:O_V], gk_ref[...], cos, sin, seg_ones, swap_lo).astype(BF16)
    v_ref[...] = z[:, O_V:].astype(BF16)


def _inproj(x2, g, w_bf, gq2, gk2, cos_t, sin_t, seq, tm=512):
    t = x2.shape[0]
    n_seq = seq // tm
    row = lambda i: (i, 0)
    fixed = lambda i: (0, 0)
    tab = lambda i: (i % n_seq, 0)
    return pl.pallas_call(
        _inproj_kernel,
        grid=(t // tm,),
        in_specs=[
            pl.BlockSpec((tm, D_MODEL), row),
            pl.BlockSpec((1, D_MODEL), fixed),
            pl.BlockSpec((D_MODEL, D_IN), fixed),
            pl.BlockSpec((1, LANES), fixed),
            pl.BlockSpec((1, LANES), fixed),
            pl.BlockSpec((tm, LANES), tab),
            pl.BlockSpec((tm, LANES), tab),
        ],
        out_specs=[
            pl.BlockSpec((FNET_WIDTH // LANES, tm, LANES), lambda i: (0, i, 0)),
            pl.BlockSpec((tm, 2 * SGU_WIDTH), row),
            pl.BlockSpec((tm, Q_WIDTH), row),
            pl.BlockSpec((tm, KV_WIDTH), row),
            pl.BlockSpec((tm, KV_WIDTH), row),
        ],
        out_shape=[
            jax.ShapeDtypeStruct((FNET_WIDTH // LANES, t, LANES), F32),
            jax.ShapeDtypeStruct((t, 2 * SGU_WIDTH), F32),
            jax.ShapeDtypeStruct((t, Q_WIDTH), BF16),
            jax.ShapeDtypeStruct((t, KV_WIDTH), BF16),
            jax.ShapeDtypeStruct((t, KV_WIDTH), BF16),
        ],
        compiler_params=_params("parallel"),
        name="inproj",
    )(x2, g, w_bf, gq2, gk2, cos_t, sin_t)


def _fourier_tables():
    n = np.arange(GRID_W)
    ang64 = 2.0 * np.pi * np.outer(n, n) / GRID_W
    c64, s64 = np.cos(ang64), np.sin(ang64)
    third = 1.0 / 8.0
    groups = FNET_WIDTH // HEAD_DIM
    eye = np.eye(groups)
    cs = np.concatenate([np.kron(eye, c64), -np.kron(eye, s64)], axis=1) * third
    f1 = np.block([[c64, s64], [-s64, c64]]) * third
    k = n[:, None] + GRID_W * n[None, :]
    ang = 2.0 * np.pi * k[:, :, None] * n[None, None, :] / (GRID_W * GRID_W)
    g2 = np.concatenate([np.cos(ang), np.sin(ang)], axis=2) * third
    return tuple(jnp.asarray(a, F32) for a in (cs, f1, g2))


def _fourier_kernel(z_ref, cs_ref, f1_ref, g2_ref, y_ref, mid_ref):
    halves = range(FNET_WIDTH // LANES)

    def stage_a(b, carry):
        rows = pl.ds(b, GRID_W, stride=GRID_W)
        xb = jnp.concatenate([z_ref[j, 0, rows, :] for j in halves], axis=-1).astype(BF16)
        ri = jnp.dot(xb, cs_ref[...].astype(BF16), preferred_element_type=F32)
        st = jnp.concatenate([ri[:, :FNET_WIDTH], ri[:, FNET_WIDTH:]], axis=0).astype(BF16)
        t = jnp.dot(f1_ref[...].astype(BF16), st, preferred_element_type=F32)
        dst = pl.ds(pl.multiple_of(b * 2 * GRID_W, 2 * GRID_W), 2 * GRID_W)
        for j in halves:
            mid_ref[j, dst, :] = t[:, j * LANES:(j + 1) * LANES]
        return carry

    lax.fori_loop(0, GRID_W, stage_a, 0)

    def stage_b(k1, carry):
        re_rows = pl.ds(k1, GRID_W, stride=2 * GRID_W)
        im_rows = pl.ds(k1 + GRID_W, GRID_W, stride=2 * GRID_W)
        tr = jnp.concatenate([mid_ref[j, re_rows, :] for j in halves], axis=-1)
        ti = jnp.concatenate([mid_ref[j, im_rows, :] for j in halves], axis=-1)
        st = jnp.concatenate([tr, ti], axis=0).astype(BF16)
        out = jnp.dot(g2_ref[k1].astype(BF16), st, preferred_element_type=F32)
        for j in halves:
            y_ref[j, 0, pl.ds(k1, GRID_W, stride=GRID_W), :] = out[:, j * LANES:(j + 1) * LANES]
        return carry

    lax.fori_loop(0, GRID_W, stage_b, 0)


def _fourier(zf4, tables):
    nh, b, s, w = zf4.shape
    cs, f1, g2 = tables
    return pl.pallas_call(
        _fourier_kernel,
        grid=(b,),
        in_specs=[
            pl.BlockSpec((nh, 1, s, w), lambda i: (0, i, 0, 0)),
            pl.BlockSpec(cs.shape, lambda i: (0, 0)),
            pl.BlockSpec(f1.shape, lambda i: (0, 0)),
            pl.BlockSpec(g2.shape, lambda i: (0, 0, 0)),
        ],
        out_specs=pl.BlockSpec((nh, 1, s, w), lambda i: (0, i, 0, 0)),
        out_shape=jax.ShapeDtypeStruct((nh, b, s, w), F32),
        scratch_shapes=[pltpu.VMEM((nh, 2 * s, w), F32)],
        compiler_params=_params("parallel"),
        name="fourier",
    )(zf4, cs, f1, g2)


def _gelu_tanh(x):
    c = math.sqrt(2.0 / math.pi)
    return 0.5 * x * (1.0 + jnp.tanh(c * (x + 0.044715 * (x * x * x))))


def _sgu_kernel(z_ref, w_ref, bias_ref, g_ref, y_ref):
    z = _gelu_tanh(z_ref[...])
    u = z[:, :SGU_WIDTH]
    v = z[:, SGU_WIDTH:]
    mu = jnp.mean(v, axis=-1, keepdims=True)
    d = v - mu
    var = jnp.mean(d * d, axis=-1, keepdims=True)
    vn = (d * lax.rsqrt(var + EPS) * g_ref[...]).astype(BF16)
    head = lax.broadcasted_iota(jnp.int32, (1, SGU_WIDTH), 1) // HEAD_DIM
    bias = bias_ref[...]
    for c in range(z.shape[0] // SGU_CHUNK):
        rows = slice(c * SGU_CHUNK, (c + 1) * SGU_CHUNK)
        vc = vn[rows]
        sv = jnp.dot(w_ref[0], vc, preferred_element_type=F32)
        for hd in range(1, SGU_HEADS):
            sv = jnp.where(head == hd, jnp.dot(w_ref[hd], vc, preferred_element_type=F32), sv)
        y_ref[rows, :] = u[rows] * (sv + bias)


def _sgu(zs2, w_bf, bias_full, g, tm=512):
    t = zs2.shape[0]
    return pl.pallas_call(
        _sgu_kernel,
        grid=(t // tm,),
        in_specs=[
            pl.BlockSpec((tm, 2 * SGU_WIDTH), lambda i: (i, 0)),
            pl.BlockSpec(w_bf.shape, lambda i: (0, 0, 0)),
            pl.BlockSpec(bias_full.shape, lambda i: (0, 0)),
            pl.BlockSpec((1, SGU_WIDTH), lambda i: (0, 0)),
        ],
        out_specs=pl.BlockSpec((tm, SGU_WIDTH), lambda i: (i, 0)),
        out_shape=jax.ShapeDtypeStruct((t, SGU_WIDTH), F32),
        compiler_params=_params("parallel"),
        name="sgu",
    )(zs2, w_bf, bias_full, g)


def _attn_kernel(q_ref, k_ref, v_ref, o_ref):
    tq = q_ref.shape[1]
    lane = lax.broadcasted_iota(jnp.int32, (1, LANES), 1)
    kblk = k_ref[0]
    vblk = v_ref[0]
    slabs = [q_ref[0, :, j * LANES:(j + 1) * LANES].astype(F32) for j in range(Q_WIDTH // LANES)]
    outs = [None] * N_Q_HEADS
    for kv in range(N_KV_HEADS):
        in_kv = (lane // HEAD_DIM) == kv
        parts = []
        for g in range(Q_PER_KV):
            hd = kv * Q_PER_KV + g
            qs = slabs[hd // 2]
            if hd % 2 != kv:
                qs = pltpu.roll(qs, HEAD_DIM, 1)
            parts.append(jnp.where(in_kv, qs, 0.0).astype(BF16))
        qst = jnp.concatenate(parts, axis=0)
        s = lax.dot_general(qst, kblk, (((1,), (1,)), ((), ())), preferred_element_type=F32)
        m = jnp.max(s, axis=-1, keepdims=True)
        p = jnp.exp(s - m)
        l = jnp.sum(p, axis=-1, keepdims=True)
        o = jnp.dot(p.astype(BF16), vblk, preferred_element_type=F32) / l
        for g in range(Q_PER_KV):
            hd = kv * Q_PER_KV + g
            og = o[g * tq:(g + 1) * tq]
            if hd % 2 != kv:
                og = pltpu.roll(og, HEAD_DIM, 1)
            outs[hd] = og
    first = lane < HEAD_DIM
    for j in range(Q_WIDTH // LANES):
        o_ref[0, :, j * LANES:(j + 1) * LANES] = jnp.where(first, outs[2 * j], outs[2 * j + 1])


def _attention(q3, k3, v3, tq=128):
    b, s, _ = q3.shape
    return pl.pallas_call(
        _attn_kernel,
        grid=(b, s // tq),
        in_specs=[
            pl.BlockSpec((1, tq, Q_WIDTH), lambda i, j: (i, j, 0)),
            pl.BlockSpec((1, s, KV_WIDTH), lambda i, j: (i, 0, 0)),
            pl.BlockSpec((1, s, KV_WIDTH), lambda i, j: (i, 0, 0)),
        ],
        out_specs=pl.BlockSpec((1, tq, Q_WIDTH), lambda i, j: (i, j, 0)),
        out_shape=jax.ShapeDtypeStruct((b, s, Q_WIDTH), F32),
        compiler_params=_params("parallel", "parallel"),
        name="attention",
    )(q3, k3, v3)


def _outproj_kernel(yf_ref, ys_ref, ya_ref, x_ref, gm_ref, w_ref, gp_ref, o_ref):
    a1 = FNET_WIDTH + SGU_WIDTH
    gm = gm_ref[...]
    yf = jnp.concatenate([yf_ref[j] for j in range(FNET_WIDTH // LANES)], axis=-1)
    y = jnp.concatenate([
        _rms(yf, gm[:, :FNET_WIDTH]),
        _rms(ys_ref[...], gm[:, FNET_WIDTH:a1]),
        _rms(ya_ref[...], gm[:, a1:]),
    ], axis=-1).astype(BF16)
    m = jnp.dot(y, w_ref[...], preferred_element_type=F32)
    o_ref[...] = x_ref[...] + _rms(m, gp_ref[...])


def _outproj(yf2, ys2, ya2, x2, gm, w_bf, gp, tm=512):
    t = x2.shape[0]
    row = lambda i: (i, 0)
    fixed = lambda i: (0, 0)
    return pl.pallas_call(
        _outproj_kernel,
        grid=(t // tm,),
        in_specs=[
            pl.BlockSpec((FNET_WIDTH // LANES, tm, LANES), lambda i: (0, i, 0)),
            pl.BlockSpec((tm, SGU_WIDTH), row),
            pl.BlockSpec((tm, Q_WIDTH), row),
            pl.BlockSpec((tm, D_MODEL), row),
            pl.BlockSpec((1, D_MIX), fixed),
            pl.BlockSpec((D_MIX, D_MODEL), fixed),
            pl.BlockSpec((1, D_MODEL), fixed),
        ],
        out_specs=pl.BlockSpec((tm, D_MODEL), row),
        out_shape=jax.ShapeDtypeStruct((t, D_MODEL), F32),
        compiler_params=_params("parallel"),
        name="outproj",
    )(yf2, ys2, ya2, x2, gm, w_bf, gp)


def _ffn_kernel(x_ref, gpre_ref, wg_ref, wu_ref, wd_ref, gpost_ref, o_ref, h_ref, acc_ref):
    j = pl.program_id(1)

    @pl.when(j == 0)
    def _():
        h_ref[...] = _rms(x_ref[...], gpre_ref[...]).astype(BF16)
        acc_ref[...] = jnp.zeros_like(acc_ref)

    h = h_ref[...]
    gate = jnp.dot(h, wg_ref[...], preferred_element_type=F32)
    up = jnp.dot(h, wu_ref[...], preferred_element_type=F32)
    act = (gate * (1.0 / (1.0 + jnp.exp(-gate))) * up).astype(BF16)
    acc_ref[...] += jnp.dot(act, wd_ref[...], preferred_element_type=F32)

    @pl.when(j == pl.num_programs(1) - 1)
    def _():
        o_ref[...] = x_ref[...] + _rms(acc_ref[...], gpost_ref[...])


def _ffn(x2, gpre, wg_bf, wu_bf, wd_bf, gpost, tm=1024, tf=256):
    t = x2.shape[0]
    d_ff = wg_bf.shape[1]
    return pl.pallas_call(
        _ffn_kernel,
        grid=(t // tm, d_ff // tf),
        in_specs=[
            pl.BlockSpec((tm, D_MODEL), lambda i, j: (i, 0)),
            pl.BlockSpec((1, D_MODEL), lambda i, j: (0, 0)),
            pl.BlockSpec((D_MODEL, tf), lambda i, j: (0, j)),
            pl.BlockSpec((D_MODEL, tf), lambda i, j: (0, j)),
            pl.BlockSpec((tf, D_MODEL), lambda i, j: (j, 0)),
            pl.BlockSpec((1, D_MODEL), lambda i, j: (0, 0)),
        ],
        out_specs=pl.BlockSpec((tm, D_MODEL), lambda i, j: (i, 0)),
        out_shape=jax.ShapeDtypeStruct((t, D_MODEL), F32),
        scratch_shapes=[pltpu.VMEM((tm, D_MODEL), BF16), pltpu.VMEM((tm, D_MODEL), F32)],
        compiler_params=_params("parallel", "arbitrary"),
        name="ffn",
    )(x2, gpre, wg_bf, wu_bf, wd_bf, gpost)


def _rope_tables(seq):
    pos = jnp.arange(seq)
    row = (pos // GRID_W).astype(F32)
    col = (pos % GRID_W).astype(F32)
    freqs = ROPE_THETA ** (-jnp.arange(ROPE_FREQS, dtype=F32) / ROPE_FREQS)
    ang_r = row[:, None] * freqs
    ang_c = col[:, None] * freqs
    cos_r, sin_r, cos_c, sin_c = jnp.cos(ang_r), jnp.sin(ang_r), jnp.cos(ang_c), jnp.sin(ang_c)
    cos_h = jnp.concatenate([cos_r, cos_r, cos_c, cos_c], axis=-1)
    sin_h = jnp.concatenate([-sin_r, sin_r, -sin_c, sin_c], axis=-1)
    reps = LANES // HEAD_DIM
    return jnp.tile(cos_h, (1, reps)), jnp.tile(sin_h, (1, reps))


@jax.jit
def kernel(x, g_pre_mix, w_in, sgu_w, sgu_b, sgu_g, g_q, g_k, g_mix, w_out,
           g_post_mix, g_pre_ffn, w_gate, w_up, w_down, g_post_ffn):
    b, s, d = x.shape
    depth = w_in.shape[0]
    t = b * s
    cos_t, sin_t = _rope_tables(s)
    ftabs = _fourier_tables()
    reps = LANES // HEAD_DIM
    x2 = x.reshape(t, d)
    for l in range(depth):
        zf, zs, q, k, v = _inproj(
            x2, g_pre_mix[l][None], w_in[l].astype(BF16),
            jnp.tile(g_q[l], reps)[None], jnp.tile(g_k[l], reps)[None], cos_t, sin_t, s)
        nh = FNET_WIDTH // LANES
        yf = _fourier(zf.reshape(nh, b, s, LANES), ftabs).reshape(nh, t, LANES)
        bias_full = jnp.repeat(sgu_b[l].T, HEAD_DIM, axis=1)
        ys = _sgu(zs, sgu_w[l].astype(BF16), bias_full, sgu_g[l][None])
        ya = _attention(q.reshape(b, s, Q_WIDTH), k.reshape(b, s, KV_WIDTH),
                        v.reshape(b, s, KV_WIDTH)).reshape(t, Q_WIDTH)
        x2 = _outproj(yf, ys, ya, x2, g_mix[l][None], w_out[l].astype(BF16), g_post_mix[l][None])
        x2 = _ffn(x2, g_pre_ffn[l][None], w_gate[l].astype(BF16), w_up[l].astype(BF16),
                  w_down[l].astype(BF16), g_post_ffn[l][None])
    return x2.reshape(b, s, d)
```

```python
import functools
import math

import numpy as np
import jax
import jax.numpy as jnp
from jax import lax
from jax.experimental import pallas as pl
from jax.experimental.pallas import tpu as pltpu

F32 = jnp.float32
BF16 = jnp.bfloat16

D_MODEL = 1024
GRID_W = 64
HEAD_DIM = 64
FNET_WIDTH = 256
SGU_WIDTH = 256
SGU_HEADS = 4
SGU_CHUNK = 128
N_Q_HEADS = 8
N_KV_HEADS = 2
Q_PER_KV = N_Q_HEADS // N_KV_HEADS
Q_WIDTH = N_Q_HEADS * HEAD_DIM
KV_WIDTH = N_KV_HEADS * HEAD_DIM
ROPE_THETA = 10000.0
ROPE_FREQS = HEAD_DIM // 4
D_MIX = FNET_WIDTH + SGU_WIDTH + Q_WIDTH
D_IN = FNET_WIDTH + 2 * SGU_WIDTH + Q_WIDTH + 2 * KV_WIDTH
EPS = 1e-6

LANES = 128
VMEM_LIMIT = 56 * 1024 * 1024

O_SGU = FNET_WIDTH
O_Q = O_SGU + 2 * SGU_WIDTH
O_K = O_Q + Q_WIDTH
O_V = O_K + KV_WIDTH


def _params(*sem):
    return pltpu.CompilerParams(dimension_semantics=sem, vmem_limit_bytes=VMEM_LIMIT)


def _rms(x, g):
    return x * lax.rsqrt(jnp.mean(x * x, axis=-1, keepdims=True) + EPS) * g


def _head_norm_rope(xs, g, cos, sin, seg_ones, swap_lo):
    ms = jnp.dot((xs * xs).astype(BF16), seg_ones, preferred_element_type=F32) * (1.0 / HEAD_DIM)
    xn = xs * lax.rsqrt(ms + EPS) * g
    sw = jnp.where(swap_lo, pltpu.roll(xn, LANES - ROPE_FREQS, 1), pltpu.roll(xn, ROPE_FREQS, 1))
    return xn * cos + sw * sin


def _inproj_kernel(x_ref, g_ref, w_ref, gq_ref, gk_ref, cos_ref, sin_ref,
                   zf_ref, zs_ref, q_ref, k_ref, v_ref):
    h = _rms(x_ref[...], g_ref[...]).astype(BF16)
    z = jnp.dot(h, w_ref[...], preferred_element_type=F32)
    for j in range(FNET_WIDTH // LANES):
        zf_ref[j] = z[:, j * LANES:(j + 1) * LANES]
    zs_ref[...] = z[:, O_SGU:O_Q]
    r = lax.broadcasted_iota(jnp.int32, (LANES, LANES), 0) // HEAD_DIM
    c = lax.broadcasted_iota(jnp.int32, (LANES, LANES), 1) // HEAD_DIM
    seg_ones = jnp.where(r == c, 1.0, 0.0).astype(BF16)
    lane = lax.broadcasted_iota(jnp.int32, (1, LANES), 1)
    swap_lo = (lane % (2 * ROPE_FREQS)) < ROPE_FREQS
    cos = cos_ref[...]
    sin = sin_ref[...]
    scale = HEAD_DIM ** -0.5 * math.log2(math.e)
    for j in range(Q_WIDTH // LANES):
        qs = z[:, O_Q + j * LANES:O_Q + (j + 1) * LANES]
        qr = _head_norm_rope(qs, gq_ref[...], cos, sin, seg_ones, swap_lo)
        q_ref[:, j * LANES:(j + 1) * LANES] = (qr * scale).astype(BF16)
    k_ref[...] = _head_norm_rope(z[:, O_K:O_V], gk_ref[...], cos, sin, seg_ones, swap_lo).astype(BF16)
    vs = z[:, O_V:]
    first = lane < HEAD_DIM
    v_ref[0] = jnp.where(first, vs, 1.0).astype(BF16)
    v_ref[1] = jnp.where(first, pltpu.roll(vs, HEAD_DIM, 1), 1.0).astype(BF16)


def _inproj(x2, g, w_bf, gq2, gk2, cos_t, sin_t, seq, tm=512):
    t = x2.shape[0]
    n_seq = seq // tm
    row = lambda i: (i, 0)
    fixed = lambda i: (0, 0)
    tab = lambda i: (i % n_seq, 0)
    return pl.pallas_call(
        _inproj_kernel,
        grid=(t // tm,),
        in_specs=[
            pl.BlockSpec((tm, D_MODEL), row),
            pl.BlockSpec((1, D_MODEL), fixed),
            pl.BlockSpec((D_MODEL, D_IN), fixed),
            pl.BlockSpec((1, LANES), fixed),
            pl.BlockSpec((1, LANES), fixed),
            pl.BlockSpec((tm, LANES), tab),
            pl.BlockSpec((tm, LANES), tab),
        ],
        out_specs=[
            pl.BlockSpec((FNET_WIDTH // LANES, tm, LANES), lambda i: (0, i, 0)),
            pl.BlockSpec((tm, 2 * SGU_WIDTH), row),
            pl.BlockSpec((tm, Q_WIDTH), row),
            pl.BlockSpec((tm, KV_WIDTH), row),
            pl.BlockSpec((N_KV_HEADS, tm, LANES), lambda i: (0, i, 0)),
        ],
        out_shape=[
            jax.ShapeDtypeStruct((FNET_WIDTH // LANES, t, LANES), F32),
            jax.ShapeDtypeStruct((t, 2 * SGU_WIDTH), F32),
            jax.ShapeDtypeStruct((t, Q_WIDTH), BF16),
            jax.ShapeDtypeStruct((t, KV_WIDTH), BF16),
            jax.ShapeDtypeStruct((N_KV_HEADS, t, LANES), BF16),
        ],
        compiler_params=_params("parallel"),
        name="inproj",
    )(x2, g, w_bf, gq2, gk2, cos_t, sin_t)


def _fourier_tables():
    n = np.arange(GRID_W)
    ang64 = 2.0 * np.pi * np.outer(n, n) / GRID_W
    c64, s64 = np.cos(ang64), np.sin(ang64)
    third = 1.0 / 8.0
    groups = FNET_WIDTH // HEAD_DIM
    eye = np.eye(groups)
    cs = np.concatenate([np.kron(eye, c64), -np.kron(eye, s64)], axis=1) * third
    f1 = np.block([[c64, s64], [-s64, c64]]) * third
    k = n[:, None] + GRID_W * n[None, :]
    ang = 2.0 * np.pi * k[:, :, None] * n[None, None, :] / (GRID_W * GRID_W)
    g2 = np.concatenate([np.cos(ang), np.sin(ang)], axis=2) * third
    return tuple(jnp.asarray(a, F32) for a in (cs, f1, g2))


FOURIER_UNROLL = 8


def _fourier_kernel(z_ref, cs_ref, f1_ref, g2_ref, y_ref, mid_ref):
    halves = range(FNET_WIDTH // LANES)

    def stage_a(b, carry):
        rows = pl.ds(b, GRID_W, stride=GRID_W)
        xb = jnp.concatenate([z_ref[j, 0, rows, :] for j in halves], axis=-1).astype(BF16)
        ri = jnp.dot(xb, cs_ref[...].astype(BF16), preferred_element_type=F32)
        st = jnp.concatenate([ri[:, :FNET_WIDTH], ri[:, FNET_WIDTH:]], axis=0).astype(BF16)
        t = jnp.dot(f1_ref[...].astype(BF16), st, preferred_element_type=F32)
        dst = pl.ds(pl.multiple_of(b * 2 * GRID_W, 2 * GRID_W), 2 * GRID_W)
        for j in halves:
            mid_ref[j, dst, :] = t[:, j * LANES:(j + 1) * LANES]
        return carry

    lax.fori_loop(0, GRID_W, stage_a, 0, unroll=FOURIER_UNROLL)

    def stage_b(k1, carry):
        re_rows = pl.ds(k1, GRID_W, stride=2 * GRID_W)
        im_rows = pl.ds(k1 + GRID_W, GRID_W, stride=2 * GRID_W)
        tr = jnp.concatenate([mid_ref[j, re_rows, :] for j in halves], axis=-1)
        ti = jnp.concatenate([mid_ref[j, im_rows, :] for j in halves], axis=-1)
        st = jnp.concatenate([tr, ti], axis=0).astype(BF16)
        out = jnp.dot(g2_ref[k1].astype(BF16), st, preferred_element_type=F32)
        for j in halves:
            y_ref[j, 0, pl.ds(k1, GRID_W, stride=GRID_W), :] = out[:, j * LANES:(j + 1) * LANES]
        return carry

    lax.fori_loop(0, GRID_W, stage_b, 0, unroll=FOURIER_UNROLL)


def _fourier(zf4, tables):
    nh, b, s, w = zf4.shape
    cs, f1, g2 = tables
    return pl.pallas_call(
        _fourier_kernel,
        grid=(b,),
        in_specs=[
            pl.BlockSpec((nh, 1, s, w), lambda i: (0, i, 0, 0)),
            pl.BlockSpec(cs.shape, lambda i: (0, 0)),
            pl.BlockSpec(f1.shape, lambda i: (0, 0)),
            pl.BlockSpec(g2.shape, lambda i: (0, 0, 0)),
        ],
        out_specs=pl.BlockSpec((nh, 1, s, w), lambda i: (0, i, 0, 0)),
        out_shape=jax.ShapeDtypeStruct((nh, b, s, w), F32),
        scratch_shapes=[pltpu.VMEM((nh, 2 * s, w), F32)],
        compiler_params=_params("parallel"),
        name="fourier",
    )(zf4, cs, f1, g2)


def _gelu_tanh(x):
    c = math.sqrt(2.0 / math.pi)
    return 0.5 * x * (1.0 + jnp.tanh(c * (x + 0.044715 * (x * x * x))))


def _sgu_kernel(z_ref, w_ref, bias_ref, g_ref, y_ref):
    z = _gelu_tanh(z_ref[...])
    u = z[:, :SGU_WIDTH]
    v = z[:, SGU_WIDTH:]
    mu = jnp.mean(v, axis=-1, keepdims=True)
    d = v - mu
    var = jnp.mean(d * d, axis=-1, keepdims=True)
    vn = (d * lax.rsqrt(var + EPS) * g_ref[...]).astype(BF16)
    head = lax.broadcasted_iota(jnp.int32, (1, SGU_WIDTH), 1) // HEAD_DIM
    bias = bias_ref[...]
    for c in range(z.shape[0] // SGU_CHUNK):
        rows = slice(c * SGU_CHUNK, (c + 1) * SGU_CHUNK)
        vc = vn[rows]
        sv = jnp.dot(w_ref[0], vc, preferred_element_type=F32)
        for hd in range(1, SGU_HEADS):
            sv = jnp.where(head == hd, jnp.dot(w_ref[hd], vc, preferred_element_type=F32), sv)
        y_ref[rows, :] = u[rows] * (sv + bias)


def _sgu(zs2, w_bf, bias_full, g, tm=512):
    t = zs2.shape[0]
    return pl.pallas_call(
        _sgu_kernel,
        grid=(t // tm,),
        in_specs=[
            pl.BlockSpec((tm, 2 * SGU_WIDTH), lambda i: (i, 0)),
            pl.BlockSpec(w_bf.shape, lambda i: (0, 0, 0)),
            pl.BlockSpec(bias_full.shape, lambda i: (0, 0)),
            pl.BlockSpec((1, SGU_WIDTH), lambda i: (0, 0)),
        ],
        out_specs=pl.BlockSpec((tm, SGU_WIDTH), lambda i: (i, 0)),
        out_shape=jax.ShapeDtypeStruct((t, SGU_WIDTH), F32),
        compiler_params=_params("parallel"),
        name="sgu",
    )(zs2, w_bf, bias_full, g)


ATTN_UNIT = 256


def _attn_kernel(q_ref, k_ref, v_ref, o_ref, qpad_ref, s0_ref, s1_ref, p0_ref, p1_ref, ost_ref):
    s_refs, p_refs = (s0_ref, s1_ref), (p0_ref, p1_ref)
    tq = q_ref.shape[1]
    seq = k_ref.shape[1]
    n_units = N_Q_HEADS * tq // ATTN_UNIT
    units_per_kv = n_units // N_KV_HEADS
    n_slabs = seq // LANES
    lane = lax.broadcasted_iota(jnp.int32, (1, LANES), 1)
    first = lane < HEAD_DIM

    for hd in range(N_Q_HEADS):
        kv = hd // Q_PER_KV
        qs = q_ref[0, :, (hd // 2) * LANES:(hd // 2 + 1) * LANES].astype(F32)
        if hd % 2 != kv:
            qs = pltpu.roll(qs, HEAD_DIM, 1)
        qpad_ref[hd * tq:(hd + 1) * tq, :] = jnp.where((lane // HEAD_DIM) == kv, qs, 0.0).astype(BF16)

    def unit_rows(u):
        start = u * ATTN_UNIT
        if not isinstance(u, int):
            start = pl.multiple_of(start, ATTN_UNIT)
        return pl.ds(start, ATTN_UNIT)

    def stage_a(u, slot):
        s_refs[slot][...] = lax.dot_general(qpad_ref[unit_rows(u), :], k_ref[0],
                                            (((1,), (1,)), ((), ())), preferred_element_type=F32)

    def stage_b(slot):
        s_ref, p_ref = s_refs[slot], p_refs[slot]
        m = s_ref[:, 0:LANES]
        for c in range(1, n_slabs):
            m = jnp.maximum(m, s_ref[:, c * LANES:(c + 1) * LANES])
        mb = jnp.broadcast_to(jnp.max(m, axis=-1, keepdims=True), (ATTN_UNIT, LANES))
        for c in range(n_slabs):
            cols = slice(c * LANES, (c + 1) * LANES)
            p_ref[:, cols] = jnp.exp2(s_ref[:, cols] - mb).astype(BF16)

    def stage_c(u, slot):
        o = jnp.dot(p_refs[slot][...], v_ref[u // units_per_kv, 0], preferred_element_type=F32)
        ost_ref[unit_rows(u), :] = o / jnp.where(first, pltpu.roll(o, HEAD_DIM, 1), 1.0)

    stage_a(0, 0)
    stage_a(1, 1)
    stage_b(0)

    def steady(i, carry):
        t = 2 * i
        stage_a(t, 0)
        stage_b(1)
        stage_c(t - 2, 0)
        stage_a(t + 1, 1)
        stage_b(0)
        stage_c(t - 1, 1)
        return carry

    lax.fori_loop(1, n_units // 2, steady, 0)
    stage_b(1)
    stage_c(n_units - 2, 0)
    stage_c(n_units - 1, 1)

    for j in range(Q_WIDTH // LANES):
        even = ost_ref[(2 * j) * tq:(2 * j + 1) * tq, :]
        odd = pltpu.roll(ost_ref[(2 * j + 1) * tq:(2 * j + 2) * tq, :], HEAD_DIM, 1)
        o_ref[0, :, j * LANES:(j + 1) * LANES] = jnp.where(first, even, odd)


def _attention(q3, k3, v4, tq=512):
    b, s, _ = q3.shape
    return pl.pallas_call(
        _attn_kernel,
        grid=(b, s // tq),
        in_specs=[
            pl.BlockSpec((1, tq, Q_WIDTH), lambda i, j: (i, j, 0)),
            pl.BlockSpec((1, s, KV_WIDTH), lambda i, j: (i, 0, 0)),
            pl.BlockSpec((N_KV_HEADS, 1, s, LANES), lambda i, j: (0, i, 0, 0)),
        ],
        out_specs=pl.BlockSpec((1, tq, Q_WIDTH), lambda i, j: (i, j, 0)),
        out_shape=jax.ShapeDtypeStruct((b, s, Q_WIDTH), F32),
        scratch_shapes=[
            pltpu.VMEM((N_Q_HEADS * tq, LANES), BF16),
            pltpu.VMEM((ATTN_UNIT, s), F32),
            pltpu.VMEM((ATTN_UNIT, s), F32),
            pltpu.VMEM((ATTN_UNIT, s), BF16),
            pltpu.VMEM((ATTN_UNIT, s), BF16),
            pltpu.VMEM((N_Q_HEADS * tq, LANES), F32),
        ],
        compiler_params=_params("parallel", "parallel"),
        name="attention",
    )(q3, k3, v4)


def _outproj_kernel(yf_ref, ys_ref, ya_ref, x_ref, gm_ref, w_ref, gp_ref, o_ref):
    a1 = FNET_WIDTH + SGU_WIDTH
    gm = gm_ref[...]
    yf = jnp.concatenate([yf_ref[j] for j in range(FNET_WIDTH // LANES)], axis=-1)
    y = jnp.concatenate([
        _rms(yf, gm[:, :FNET_WIDTH]),
        _rms(ys_ref[...], gm[:, FNET_WIDTH:a1]),
        _rms(ya_ref[...], gm[:, a1:]),
    ], axis=-1).astype(BF16)
    m = jnp.dot(y, w_ref[...], preferred_element_type=F32)
    o_ref[...] = x_ref[...] + _rms(m, gp_ref[...])


def _outproj(yf2, ys2, ya2, x2, gm, w_bf, gp, tm=512):
    t = x2.shape[0]
    row = lambda i: (i, 0)
    fixed = lambda i: (0, 0)
    return pl.pallas_call(
        _outproj_kernel,
        grid=(t // tm,),
        in_specs=[
            pl.BlockSpec((FNET_WIDTH // LANES, tm, LANES), lambda i: (0, i, 0)),
            pl.BlockSpec((tm, SGU_WIDTH), row),
            pl.BlockSpec((tm, Q_WIDTH), row),
            pl.BlockSpec((tm, D_MODEL), row),
            pl.BlockSpec((1, D_MIX), fixed),
            pl.BlockSpec((D_MIX, D_MODEL), fixed),
            pl.BlockSpec((1, D_MODEL), fixed),
        ],
        out_specs=pl.BlockSpec((tm, D_MODEL), row),
        out_shape=jax.ShapeDtypeStruct((t, D_MODEL), F32),
        compiler_params=_params("parallel"),
        name="outproj",
    )(yf2, ys2, ya2, x2, gm, w_bf, gp)


def _ffn_kernel(x_ref, gpre_ref, wg_ref, wu_ref, wd_ref, gpost_ref, o_ref, h_ref, acc_ref):
    j = pl.program_id(1)

    @pl.when(j == 0)
    def _():
        h_ref[...] = _rms(x_ref[...], gpre_ref[...]).astype(BF16)
        acc_ref[...] = jnp.zeros_like(acc_ref)

    h = h_ref[...]
    gate = jnp.dot(h, wg_ref[...], preferred_element_type=F32)
    up = jnp.dot(h, wu_ref[...], preferred_element_type=F32)
    act = (gate * (1.0 / (1.0 + jnp.exp(-gate))) * up).astype(BF16)
    acc_ref[...] += jnp.dot(act, wd_ref[...], preferred_element_type=F32)

    @pl.when(j == pl.num_programs(1) - 1)
    def _():
        o_ref[...] = x_ref[...] + _rms(acc_ref[...], gpost_ref[...])


def _ffn(x2, gpre, wg_bf, wu_bf, wd_bf, gpost, tm=1024, tf=256):
    t = x2.shape[0]
    d_ff = wg_bf.shape[1]
    return pl.pallas_call(
        _ffn_kernel,
        grid=(t // tm, d_ff // tf),
        in_specs=[
            pl.BlockSpec((tm, D_MODEL), lambda i, j: (i, 0)),
            pl.BlockSpec((1, D_MODEL), lambda i, j: (0, 0)),
            pl.BlockSpec((D_MODEL, tf), lambda i, j: (0, j)),
            pl.BlockSpec((D_MODEL, tf), lambda i, j: (0, j)),
            pl.BlockSpec((tf, D_MODEL), lambda i, j: (j, 0)),
            pl.BlockSpec((1, D_MODEL), lambda i, j: (0, 0)),
        ],
        out_specs=pl.BlockSpec((tm, D_MODEL), lambda i, j: (i, 0)),
        out_shape=jax.ShapeDtypeStruct((t, D_MODEL), F32),
        scratch_shapes=[pltpu.VMEM((tm, D_MODEL), BF16), pltpu.VMEM((tm, D_MODEL), F32)],
        compiler_params=_params("parallel", "arbitrary"),
        name="ffn",
    )(x2, gpre, wg_bf, wu_bf, wd_bf, gpost)


def _rope_tables(seq):
    pos = jnp.arange(seq)
    row = (pos // GRID_W).astype(F32)
    col = (pos % GRID_W).astype(F32)
    freqs = ROPE_THETA ** (-jnp.arange(ROPE_FREQS, dtype=F32) / ROPE_FREQS)
    ang_r = row[:, None] * freqs
    ang_c = col[:, None] * freqs
    cos_r, sin_r, cos_c, sin_c = jnp.cos(ang_r), jnp.sin(ang_r), jnp.cos(ang_c), jnp.sin(ang_c)
    cos_h = jnp.concatenate([cos_r, cos_r, cos_c, cos_c], axis=-1)
    sin_h = jnp.concatenate([-sin_r, sin_r, -sin_c, sin_c], axis=-1)
    reps = LANES // HEAD_DIM
    return jnp.tile(cos_h, (1, reps)), jnp.tile(sin_h, (1, reps))


@jax.jit
def kernel(x, g_pre_mix, w_in, sgu_w, sgu_b, sgu_g, g_q, g_k, g_mix, w_out,
           g_post_mix, g_pre_ffn, w_gate, w_up, w_down, g_post_ffn):
    b, s, d = x.shape
    depth = w_in.shape[0]
    t = b * s
    cos_t, sin_t = _rope_tables(s)
    ftabs = _fourier_tables()
    reps = LANES // HEAD_DIM
    x2 = x.reshape(t, d)
    for l in range(depth):
        zf, zs, q, k, v = _inproj(
            x2, g_pre_mix[l][None], w_in[l].astype(BF16),
            jnp.tile(g_q[l], reps)[None], jnp.tile(g_k[l], reps)[None], cos_t, sin_t, s)
        nh = FNET_WIDTH // LANES
        yf = _fourier(zf.reshape(nh, b, s, LANES), ftabs).reshape(nh, t, LANES)
        bias_full = jnp.repeat(sgu_b[l].T, HEAD_DIM, axis=1)
        ys = _sgu(zs, sgu_w[l].astype(BF16), bias_full, sgu_g[l][None])
        ya = _attention(q.reshape(b, s, Q_WIDTH), k.reshape(b, s, KV_WIDTH),
                        v.reshape(N_KV_HEADS, b, s, LANES)).reshape(t, Q_WIDTH)
        x2 = _outproj(yf, ys, ya, x2, g_mix[l][None], w_out[l].astype(BF16), g_post_mix[l][None])
        x2 = _ffn(x2, g_pre_ffn[l][None], w_gate[l].astype(BF16), w_up[l].astype(BF16),
                  w_down[l].astype(BF16), g_post_ffn[l][None])
    return x2.reshape(b, s, d)
```

```python
import functools
import math

import numpy as np
import jax
import jax.numpy as jnp
from jax import lax
from jax.experimental import pallas as pl
from jax.experimental.pallas import tpu as pltpu

F32 = jnp.float32
BF16 = jnp.bfloat16

D_MODEL = 1024
GRID_W = 64
HEAD_DIM = 64
FNET_WIDTH = 256
SGU_WIDTH = 256
SGU_HEADS = 4
SGU_CHUNK = 128
N_Q_HEADS = 8
N_KV_HEADS = 2
Q_PER_KV = N_Q_HEADS // N_KV_HEADS
Q_WIDTH = N_Q_HEADS * HEAD_DIM
KV_WIDTH = N_KV_HEADS * HEAD_DIM
ROPE_THETA = 10000.0
ROPE_FREQS = HEAD_DIM // 4
D_MIX = FNET_WIDTH + SGU_WIDTH + Q_WIDTH
D_IN = FNET_WIDTH + 2 * SGU_WIDTH + Q_WIDTH + 2 * KV_WIDTH
EPS = 1e-6

LANES = 128
SUBLANES = 8
BF16_SUBLANES = 16
VT_ROWS = HEAD_DIM + BF16_SUBLANES
VMEM_LIMIT = 56 * 1024 * 1024

O_SGU = FNET_WIDTH
O_Q = O_SGU + 2 * SGU_WIDTH
O_K = O_Q + Q_WIDTH
O_V = O_K + KV_WIDTH


def _params(*sem):
    return pltpu.CompilerParams(dimension_semantics=sem, vmem_limit_bytes=VMEM_LIMIT)


def _rms(x, g):
    return x * lax.rsqrt(jnp.mean(x * x, axis=-1, keepdims=True) + EPS) * g


def _head_norm_rope(xs, g, cos, sin, seg_ones, swap_lo):
    ms = jnp.dot((xs * xs).astype(BF16), seg_ones, preferred_element_type=F32) * (1.0 / HEAD_DIM)
    xn = xs * lax.rsqrt(ms + EPS) * g
    sw = jnp.where(swap_lo, pltpu.roll(xn, LANES - ROPE_FREQS, 1), pltpu.roll(xn, ROPE_FREQS, 1))
    return xn * cos + sw * sin


def _inproj_kernel(x_ref, g_ref, w_ref, gq_ref, gk_ref, cos_ref, sin_ref,
                   zf_ref, zs_ref, q_ref, k_ref, vt_ref):
    h = _rms(x_ref[...], g_ref[...]).astype(BF16)
    z = jnp.dot(h, w_ref[...], preferred_element_type=F32)
    for j in range(FNET_WIDTH // LANES):
        zf_ref[j] = z[:, j * LANES:(j + 1) * LANES]
    zs_ref[...] = z[:, O_SGU:O_Q]
    r = lax.broadcasted_iota(jnp.int32, (LANES, LANES), 0) // HEAD_DIM
    c = lax.broadcasted_iota(jnp.int32, (LANES, LANES), 1) // HEAD_DIM
    seg_ones = jnp.where(r == c, 1.0, 0.0).astype(BF16)
    lane = lax.broadcasted_iota(jnp.int32, (1, LANES), 1)
    swap_lo = (lane % (2 * ROPE_FREQS)) < ROPE_FREQS
    cos = cos_ref[...]
    sin = sin_ref[...]
    scale = HEAD_DIM ** -0.5 * math.log2(math.e)
    for j in range(Q_WIDTH // LANES):
        qs = z[:, O_Q + j * LANES:O_Q + (j + 1) * LANES]
        qr = _head_norm_rope(qs, gq_ref[...], cos, sin, seg_ones, swap_lo)
        q_ref[:, j * LANES:(j + 1) * LANES] = (qr * scale).astype(BF16)
    k_ref[...] = _head_norm_rope(z[:, O_K:O_V], gk_ref[...], cos, sin, seg_ones, swap_lo).astype(BF16)
    vt = z[:, O_V:].T
    ones = jnp.ones((VT_ROWS - HEAD_DIM, vt.shape[1]), BF16)
    for kv in range(N_KV_HEADS):
        vt_ref[0, kv, :HEAD_DIM, :] = vt[kv * HEAD_DIM:(kv + 1) * HEAD_DIM].astype(BF16)
        vt_ref[0, kv, HEAD_DIM:, :] = ones


def _inproj(x2, g, w_bf, gq2, gk2, cos_t, sin_t, seq, tm=512):
    t = x2.shape[0]
    n_seq = seq // tm
    row = lambda i: (i, 0)
    fixed = lambda i: (0, 0)
    tab = lambda i: (i % n_seq, 0)
    return pl.pallas_call(
        _inproj_kernel,
        grid=(t // tm,),
        in_specs=[
            pl.BlockSpec((tm, D_MODEL), row),
            pl.BlockSpec((1, D_MODEL), fixed),
            pl.BlockSpec((D_MODEL, D_IN), fixed),
            pl.BlockSpec((1, LANES), fixed),
            pl.BlockSpec((1, LANES), fixed),
            pl.BlockSpec((tm, LANES), tab),
            pl.BlockSpec((tm, LANES), tab),
        ],
        out_specs=[
            pl.BlockSpec((FNET_WIDTH // LANES, tm, LANES), lambda i: (0, i, 0)),
            pl.BlockSpec((tm, 2 * SGU_WIDTH), row),
            pl.BlockSpec((tm, Q_WIDTH), row),
            pl.BlockSpec((tm, KV_WIDTH), row),
            pl.BlockSpec((1, N_KV_HEADS, VT_ROWS, tm), lambda i: (i // n_seq, 0, 0, i % n_seq)),
        ],
        out_shape=[
            jax.ShapeDtypeStruct((FNET_WIDTH // LANES, t, LANES), F32),
            jax.ShapeDtypeStruct((t, 2 * SGU_WIDTH), F32),
            jax.ShapeDtypeStruct((t, Q_WIDTH), BF16),
            jax.ShapeDtypeStruct((t, KV_WIDTH), BF16),
            jax.ShapeDtypeStruct((t // seq, N_KV_HEADS, VT_ROWS, seq), BF16),
        ],
        compiler_params=_params("parallel"),
        name="inproj",
    )(x2, g, w_bf, gq2, gk2, cos_t, sin_t)


def _fourier_tables():
    n = np.arange(GRID_W)
    ang64 = 2.0 * np.pi * np.outer(n, n) / GRID_W
    c64, s64 = np.cos(ang64), np.sin(ang64)
    third = 1.0 / 8.0
    groups = FNET_WIDTH // HEAD_DIM
    eye = np.eye(groups)
    cs = np.concatenate([np.kron(eye, c64), -np.kron(eye, s64)], axis=1) * third
    f1 = np.block([[c64, s64], [-s64, c64]]) * third
    k = n[:, None] + GRID_W * n[None, :]
    ang = 2.0 * np.pi * k[:, :, None] * n[None, None, :] / (GRID_W * GRID_W)
    g2 = np.concatenate([np.cos(ang), np.sin(ang)], axis=2) * third
    return tuple(jnp.asarray(a, F32) for a in (cs, f1, g2))


FOURIER_UNROLL = 8


def _fourier_kernel(z_ref, cs_ref, f1_ref, g2_ref, y_ref, mid_ref):
    halves = range(FNET_WIDTH // LANES)

    def stage_a(b, carry):
        rows = pl.ds(b, GRID_W, stride=GRID_W)
        xb = jnp.concatenate([z_ref[j, 0, rows, :] for j in halves], axis=-1).astype(BF16)
        ri = jnp.dot(xb, cs_ref[...].astype(BF16), preferred_element_type=F32)
        st = jnp.concatenate([ri[:, :FNET_WIDTH], ri[:, FNET_WIDTH:]], axis=0).astype(BF16)
        t = jnp.dot(f1_ref[...].astype(BF16), st, preferred_element_type=F32)
        dst = pl.ds(pl.multiple_of(b * 2 * GRID_W, 2 * GRID_W), 2 * GRID_W)
        for j in halves:
            mid_ref[j, dst, :] = t[:, j * LANES:(j + 1) * LANES]
        return carry

    lax.fori_loop(0, GRID_W, stage_a, 0, unroll=FOURIER_UNROLL)

    def stage_b(k1, carry):
        re_rows = pl.ds(k1, GRID_W, stride=2 * GRID_W)
        im_rows = pl.ds(k1 + GRID_W, GRID_W, stride=2 * GRID_W)
        tr = jnp.concatenate([mid_ref[j, re_rows, :] for j in halves], axis=-1)
        ti = jnp.concatenate([mid_ref[j, im_rows, :] for j in halves], axis=-1)
        st = jnp.concatenate([tr, ti], axis=0).astype(BF16)
        out = jnp.dot(g2_ref[k1].astype(BF16), st, preferred_element_type=F32)
        for j in halves:
            y_ref[j, 0, pl.ds(k1, GRID_W, stride=GRID_W), :] = out[:, j * LANES:(j + 1) * LANES]
        return carry

    lax.fori_loop(0, GRID_W, stage_b, 0, unroll=FOURIER_UNROLL)


def _fourier(zf4, tables):
    nh, b, s, w = zf4.shape
    cs, f1, g2 = tables
    return pl.pallas_call(
        _fourier_kernel,
        grid=(b,),
        in_specs=[
            pl.BlockSpec((nh, 1, s, w), lambda i: (0, i, 0, 0)),
            pl.BlockSpec(cs.shape, lambda i: (0, 0)),
            pl.BlockSpec(f1.shape, lambda i: (0, 0)),
            pl.BlockSpec(g2.shape, lambda i: (0, 0, 0)),
        ],
        out_specs=pl.BlockSpec((nh, 1, s, w), lambda i: (0, i, 0, 0)),
        out_shape=jax.ShapeDtypeStruct((nh, b, s, w), F32),
        scratch_shapes=[pltpu.VMEM((nh, 2 * s, w), F32)],
        compiler_params=_params("parallel"),
        name="fourier",
    )(zf4, cs, f1, g2)


def _gelu_tanh(x):
    c = math.sqrt(2.0 / math.pi)
    return 0.5 * x * (1.0 + jnp.tanh(c * (x + 0.044715 * (x * x * x))))


def _sgu_kernel(z_ref, w_ref, bias_ref, g_ref, y_ref):
    z = _gelu_tanh(z_ref[...])
    u = z[:, :SGU_WIDTH]
    v = z[:, SGU_WIDTH:]
    mu = jnp.mean(v, axis=-1, keepdims=True)
    d = v - mu
    var = jnp.mean(d * d, axis=-1, keepdims=True)
    vn = (d * lax.rsqrt(var + EPS) * g_ref[...]).astype(BF16)
    head = lax.broadcasted_iota(jnp.int32, (1, SGU_WIDTH), 1) // HEAD_DIM
    bias = bias_ref[...]
    for c in range(z.shape[0] // SGU_CHUNK):
        rows = slice(c * SGU_CHUNK, (c + 1) * SGU_CHUNK)
        vc = vn[rows]
        sv = jnp.dot(w_ref[0], vc, preferred_element_type=F32)
        for hd in range(1, SGU_HEADS):
            sv = jnp.where(head == hd, jnp.dot(w_ref[hd], vc, preferred_element_type=F32), sv)
        y_ref[rows, :] = u[rows] * (sv + bias)


def _sgu(zs2, w_bf, bias_full, g, tm=512):
    t = zs2.shape[0]
    return pl.pallas_call(
        _sgu_kernel,
        grid=(t // tm,),
        in_specs=[
            pl.BlockSpec((tm, 2 * SGU_WIDTH), lambda i: (i, 0)),
            pl.BlockSpec(w_bf.shape, lambda i: (0, 0, 0)),
            pl.BlockSpec(bias_full.shape, lambda i: (0, 0)),
            pl.BlockSpec((1, SGU_WIDTH), lambda i: (0, 0)),
        ],
        out_specs=pl.BlockSpec((tm, SGU_WIDTH), lambda i: (i, 0)),
        out_shape=jax.ShapeDtypeStruct((t, SGU_WIDTH), F32),
        compiler_params=_params("parallel"),
        name="sgu",
    )(zs2, w_bf, bias_full, g)


ATTN_UNIT = 256
ATTN_CHUNK = 64
ATTN_PIECES = 8


def _attn_kernel(q_ref, k_ref, vt_ref, o_ref, qpad_ref, s0_ref, s1_ref, p0_ref, p1_ref,
                 m0_ref, m1_ref, ot_ref):
    s_refs, p_refs, m_refs = (s0_ref, s1_ref), (p0_ref, p1_ref), (m0_ref, m1_ref)
    tq = q_ref.shape[1]
    seq = k_ref.shape[1]
    units_per_head = tq // ATTN_UNIT
    n_units = N_Q_HEADS * units_per_head
    units_per_kv = n_units // N_KV_HEADS
    lane = lax.broadcasted_iota(jnp.int32, (1, LANES), 1)

    for hd in range(N_Q_HEADS):
        kv = hd // Q_PER_KV
        qs = q_ref[0, :, (hd // 2) * LANES:(hd // 2 + 1) * LANES].astype(F32)
        if hd % 2 != kv:
            qs = pltpu.roll(qs, HEAD_DIM, 1)
        qpad_ref[hd * tq:(hd + 1) * tq, :] = jnp.where((lane // HEAD_DIM) == kv, qs, 0.0).astype(BF16)

    def unit_rows(u):
        return slice(u * ATTN_UNIT, (u + 1) * ATTN_UNIT)

    piece = seq // ATTN_PIECES

    def stage_a(u, slot):
        q_u = qpad_ref[unit_rows(u), :]
        acc = []

        def step(i):
            rows = slice(i * piece, (i + 1) * piece)
            s = lax.dot_general(k_ref[0, rows, :], q_u, (((1,), (1,)), ((), ())),
                                preferred_element_type=F32)
            s_refs[slot][rows, :] = s
            parts = [s[r:r + SUBLANES] for r in range(0, piece, SUBLANES)]
            while len(parts) > 1:
                parts = [jnp.maximum(parts[j], parts[j + 1]) for j in range(0, len(parts), 2)]
            acc.append(parts[0])

        def finish():
            m = functools.reduce(jnp.maximum, acc)
            m_refs[slot][...] = jnp.broadcast_to(jnp.max(m, axis=0, keepdims=True), m.shape)

        return [functools.partial(step, i) for i in range(ATTN_PIECES)], finish

    def stage_b(slot):
        s_ref, p_ref = s_refs[slot], p_refs[slot]

        def step(i):
            mb = jnp.broadcast_to(m_refs[slot][0:1, :], (ATTN_CHUNK, ATTN_UNIT))
            for c in range(i * piece // ATTN_CHUNK, (i + 1) * piece // ATTN_CHUNK):
                rows = slice(c * ATTN_CHUNK, (c + 1) * ATTN_CHUNK)
                p_ref[rows, :] = jnp.exp2(s_ref[rows, :] - mb).astype(BF16)

        return [functools.partial(step, i) for i in range(ATTN_PIECES)], None

    def stage_c(u, slot):
        acc = []

        def step(i):
            keys = slice(i * piece, (i + 1) * piece)
            acc.append(jnp.dot(vt_ref[0, u // units_per_kv, :, keys], p_refs[slot][keys, :],
                               preferred_element_type=F32))

        def finish():
            o = functools.reduce(jnp.add, acc)
            ot_ref[u] = o[:HEAD_DIM] / o[HEAD_DIM:HEAD_DIM + 1]

        return [functools.partial(step, i) for i in range(ATTN_PIECES)], finish

    def issue(*stages):
        for steps in zip(*[st[0] for st in stages]):
            for step in steps:
                step()
        for st in stages:
            if st[1] is not None:
                st[1]()

    issue(stage_a(0, 0))
    issue(stage_a(1, 1), stage_b(0))
    for t in range(2, n_units):
        issue(stage_a(t, t % 2), stage_b((t - 1) % 2), stage_c(t - 2, t % 2))
    issue(stage_b((n_units - 1) % 2), stage_c(n_units - 2, n_units % 2))
    issue(stage_c(n_units - 1, (n_units - 1) % 2))

    for j in range(Q_WIDTH // LANES):
        for r in range(units_per_head):
            pair = jnp.concatenate([ot_ref[(2 * j) * units_per_head + r],
                                    ot_ref[(2 * j + 1) * units_per_head + r]], axis=0)
            o_ref[0, r * ATTN_UNIT:(r + 1) * ATTN_UNIT, j * LANES:(j + 1) * LANES] = pair.T


def _attention(q3, k3, vt4, tq=512):
    b, s, _ = q3.shape
    n_units = N_Q_HEADS * tq // ATTN_UNIT
    return pl.pallas_call(
        _attn_kernel,
        grid=(b, s // tq),
        in_specs=[
            pl.BlockSpec((1, tq, Q_WIDTH), lambda i, j: (i, j, 0)),
            pl.BlockSpec((1, s, KV_WIDTH), lambda i, j: (i, 0, 0)),
            pl.BlockSpec((1, N_KV_HEADS, VT_ROWS, s), lambda i, j: (i, 0, 0, 0)),
        ],
        out_specs=pl.BlockSpec((1, tq, Q_WIDTH), lambda i, j: (i, j, 0)),
        out_shape=jax.ShapeDtypeStruct((b, s, Q_WIDTH), F32),
        scratch_shapes=[
            pltpu.VMEM((N_Q_HEADS * tq, LANES), BF16),
            pltpu.VMEM((s, ATTN_UNIT), F32),
            pltpu.VMEM((s, ATTN_UNIT), F32),
            pltpu.VMEM((s, ATTN_UNIT), BF16),
            pltpu.VMEM((s, ATTN_UNIT), BF16),
            pltpu.VMEM((SUBLANES, ATTN_UNIT), F32),
            pltpu.VMEM((SUBLANES, ATTN_UNIT), F32),
            pltpu.VMEM((n_units, HEAD_DIM, ATTN_UNIT), F32),
        ],
        compiler_params=_params("parallel", "parallel"),
        name="attention",
    )(q3, k3, vt4)


def _outproj_kernel(yf_ref, ys_ref, ya_ref, x_ref, gm_ref, w_ref, gp_ref, o_ref):
    a1 = FNET_WIDTH + SGU_WIDTH
    gm = gm_ref[...]
    yf = jnp.concatenate([yf_ref[j] for j in range(FNET_WIDTH // LANES)], axis=-1)
    y = jnp.concatenate([
        _rms(yf, gm[:, :FNET_WIDTH]),
        _rms(ys_ref[...], gm[:, FNET_WIDTH:a1]),
        _rms(ya_ref[...], gm[:, a1:]),
    ], axis=-1).astype(BF16)
    m = jnp.dot(y, w_ref[...], preferred_element_type=F32)
    o_ref[...] = x_ref[...] + _rms(m, gp_ref[...])


def _outproj(yf2, ys2, ya2, x2, gm, w_bf, gp, tm=512):
    t = x2.shape[0]
    row = lambda i: (i, 0)
    fixed = lambda i: (0, 0)
    return pl.pallas_call(
        _outproj_kernel,
        grid=(t // tm,),
        in_specs=[
            pl.BlockSpec((FNET_WIDTH // LANES, tm, LANES), lambda i: (0, i, 0)),
            pl.BlockSpec((tm, SGU_WIDTH), row),
            pl.BlockSpec((tm, Q_WIDTH), row),
            pl.BlockSpec((tm, D_MODEL), row),
            pl.BlockSpec((1, D_MIX), fixed),
            pl.BlockSpec((D_MIX, D_MODEL), fixed),
            pl.BlockSpec((1, D_MODEL), fixed),
        ],
        out_specs=pl.BlockSpec((tm, D_MODEL), row),
        out_shape=jax.ShapeDtypeStruct((t, D_MODEL), F32),
        compiler_params=_params("parallel"),
        name="outproj",
    )(yf2, ys2, ya2, x2, gm, w_bf, gp)


def _ffn_kernel(x_ref, gpre_ref, wg_ref, wu_ref, wd_ref, gpost_ref, o_ref, h_ref, acc_ref):
    j = pl.program_id(1)

    @pl.when(j == 0)
    def _():
        h_ref[...] = _rms(x_ref[...], gpre_ref[...]).astype(BF16)
        acc_ref[...] = jnp.zeros_like(acc_ref)

    h = h_ref[...]
    gate = jnp.dot(h, wg_ref[...], preferred_element_type=F32)
    up = jnp.dot(h, wu_ref[...], preferred_element_type=F32)
    act = (gate * (1.0 / (1.0 + jnp.exp(-gate))) * up).astype(BF16)
    acc_ref[...] += jnp.dot(act, wd_ref[...], preferred_element_type=F32)

    @pl.when(j == pl.num_programs(1) - 1)
    def _():
        o_ref[...] = x_ref[...] + _rms(acc_ref[...], gpost_ref[...])


def _ffn(x2, gpre, wg_bf, wu_bf, wd_bf, gpost, tm=1024, tf=256):
    t = x2.shape[0]
    d_ff = wg_bf.shape[1]
    return pl.pallas_call(
        _ffn_kernel,
        grid=(t // tm, d_ff // tf),
        in_specs=[
            pl.BlockSpec((tm, D_MODEL), lambda i, j: (i, 0)),
            pl.BlockSpec((1, D_MODEL), lambda i, j: (0, 0)),
            pl.BlockSpec((D_MODEL, tf), lambda i, j: (0, j)),
            pl.BlockSpec((D_MODEL, tf), lambda i, j: (0, j)),
            pl.BlockSpec((tf, D_MODEL), lambda i, j: (j, 0)),
            pl.BlockSpec((1, D_MODEL), lambda i, j: (0, 0)),
        ],
        out_specs=pl.BlockSpec((tm, D_MODEL), lambda i, j: (i, 0)),
        out_shape=jax.ShapeDtypeStruct((t, D_MODEL), F32),
        scratch_shapes=[pltpu.VMEM((tm, D_MODEL), BF16), pltpu.VMEM((tm, D_MODEL), F32)],
        compiler_params=_params("parallel", "arbitrary"),
        name="ffn",
    )(x2, gpre, wg_bf, wu_bf, wd_bf, gpost)


def _rope_tables(seq):
    pos = jnp.arange(seq)
    row = (pos // GRID_W).astype(F32)
    col = (pos % GRID_W).astype(F32)
    freqs = ROPE_THETA ** (-jnp.arange(ROPE_FREQS, dtype=F32) / ROPE_FREQS)
    ang_r = row[:, None] * freqs
    ang_c = col[:, None] * freqs
    cos_r, sin_r, cos_c, sin_c = jnp.cos(ang_r), jnp.sin(ang_r), jnp.cos(ang_c), jnp.sin(ang_c)
    cos_h = jnp.concatenate([cos_r, cos_r, cos_c, cos_c], axis=-1)
    sin_h = jnp.concatenate([-sin_r, sin_r, -sin_c, sin_c], axis=-1)
    reps = LANES // HEAD_DIM
    return jnp.tile(cos_h, (1, reps)), jnp.tile(sin_h, (1, reps))


@jax.jit
def kernel(x, g_pre_mix, w_in, sgu_w, sgu_b, sgu_g, g_q, g_k, g_mix, w_out,
           g_post_mix, g_pre_ffn, w_gate, w_up, w_down, g_post_ffn):
    b, s, d = x.shape
    depth = w_in.shape[0]
    t = b * s
    cos_t, sin_t = _rope_tables(s)
    ftabs = _fourier_tables()
    reps = LANES // HEAD_DIM
    x2 = x.reshape(t, d)
    for l in range(depth):
        zf, zs, q, k, v = _inproj(
            x2, g_pre_mix[l][None], w_in[l].astype(BF16),
            jnp.tile(g_q[l], reps)[None], jnp.tile(g_k[l], reps)[None], cos_t, sin_t, s)
        nh = FNET_WIDTH // LANES
        yf = _fourier(zf.reshape(nh, b, s, LANES), ftabs).reshape(nh, t, LANES)
        bias_full = jnp.repeat(sgu_b[l].T, HEAD_DIM, axis=1)
        ys = _sgu(zs, sgu_w[l].astype(BF16), bias_full, sgu_g[l][None])
        ya = _attention(q.reshape(b, s, Q_WIDTH), k.reshape(b, s, KV_WIDTH),
                        v).reshape(t, Q_WIDTH)
        x2 = _outproj(yf, ys, ya, x2, g_mix[l][None], w_out[l].astype(BF16), g_post_mix[l][None])
        x2 = _ffn(x2, g_pre_ffn[l][None], w_gate[l].astype(BF16), w_up[l].astype(BF16),
                  w_down[l].astype(BF16), g_post_ffn[l][None])
    return x2.reshape(b, s, d)
```

```python
import functools
import math

import numpy as np
import jax
import jax.numpy as jnp
from jax import lax
from jax.experimental import pallas as pl
from jax.experimental.pallas import tpu as pltpu

F32 = jnp.float32
BF16 = jnp.bfloat16

D_MODEL = 1024
GRID_W = 64
HEAD_DIM = 64
FNET_WIDTH = 256
SGU_WIDTH = 256
SGU_HEADS = 4
SGU_CHUNK = 128
N_Q_HEADS = 8
N_KV_HEADS = 2
Q_PER_KV = N_Q_HEADS // N_KV_HEADS
Q_WIDTH = N_Q_HEADS * HEAD_DIM
KV_WIDTH = N_KV_HEADS * HEAD_DIM
ROPE_THETA = 10000.0
ROPE_FREQS = HEAD_DIM // 4
D_MIX = FNET_WIDTH + SGU_WIDTH + Q_WIDTH
D_IN = FNET_WIDTH + 2 * SGU_WIDTH + Q_WIDTH + 2 * KV_WIDTH
EPS = 1e-6

LANES = 128
SUBLANES = 8
BF16_SUBLANES = 16
VT_ROWS = HEAD_DIM + BF16_SUBLANES
VMEM_LIMIT = 56 * 1024 * 1024

O_SGU = FNET_WIDTH
O_Q = O_SGU + 2 * SGU_WIDTH
O_K = O_Q + Q_WIDTH
O_V = O_K + KV_WIDTH


def _params(*sem):
    return pltpu.CompilerParams(dimension_semantics=sem, vmem_limit_bytes=VMEM_LIMIT)


ROW_SPLIT = 256


def _rms(x, g):
    return x * lax.rsqrt(jnp.mean(x * x, axis=-1, keepdims=True) + EPS) * g


def _head_norm_rope(xs, g, cos, sin, seg_ones, swap_lo):
    ms = jnp.dot((xs * xs).astype(BF16), seg_ones, preferred_element_type=F32) * (1.0 / HEAD_DIM)
    xn = xs * lax.rsqrt(ms + EPS) * g
    sw = jnp.where(swap_lo, pltpu.roll(xn, LANES - ROPE_FREQS, 1), pltpu.roll(xn, ROPE_FREQS, 1))
    return xn * cos + sw * sin


def _inproj_kernel(x_ref, g_ref, w_ref, gq_ref, gk_ref, cos_ref, sin_ref,
                   zf_ref, zs_ref, q_ref, k_ref, vt_ref):
    r = lax.broadcasted_iota(jnp.int32, (LANES, LANES), 0) // HEAD_DIM
    c = lax.broadcasted_iota(jnp.int32, (LANES, LANES), 1) // HEAD_DIM
    seg_ones = jnp.where(r == c, 1.0, 0.0).astype(BF16)
    lane = lax.broadcasted_iota(jnp.int32, (1, LANES), 1)
    swap_lo = (lane % (2 * ROPE_FREQS)) < ROPE_FREQS
    scale = HEAD_DIM ** -0.5 * math.log2(math.e)
    ones = jnp.ones((VT_ROWS - HEAD_DIM, ROW_SPLIT), BF16)
    for rs in range(x_ref.shape[0] // ROW_SPLIT):
        rows = slice(rs * ROW_SPLIT, (rs + 1) * ROW_SPLIT)
        h = _rms(x_ref[rows, :], g_ref[...]).astype(BF16)
        z = jnp.dot(h, w_ref[...], preferred_element_type=F32)
        for j in range(FNET_WIDTH // LANES):
            zf_ref[j, rows, :] = z[:, j * LANES:(j + 1) * LANES]
        zs_ref[rows, :] = z[:, O_SGU:O_Q]
        cos = cos_ref[rows, :]
        sin = sin_ref[rows, :]
        for j in range(Q_WIDTH // LANES):
            qs = z[:, O_Q + j * LANES:O_Q + (j + 1) * LANES]
            qr = _head_norm_rope(qs, gq_ref[...], cos, sin, seg_ones, swap_lo)
            q_ref[rows, j * LANES:(j + 1) * LANES] = (qr * scale).astype(BF16)
        k_ref[rows, :] = _head_norm_rope(z[:, O_K:O_V], gk_ref[...], cos, sin, seg_ones,
                                         swap_lo).astype(BF16)
        vt = z[:, O_V:].T
        for kv in range(N_KV_HEADS):
            vt_ref[0, kv, :HEAD_DIM, rows] = vt[kv * HEAD_DIM:(kv + 1) * HEAD_DIM].astype(BF16)
            vt_ref[0, kv, HEAD_DIM:, rows] = ones


def _inproj(x2, g, w_bf, gq2, gk2, cos_t, sin_t, seq, tm=1024):
    t = x2.shape[0]
    n_seq = seq // tm
    row = lambda i: (i, 0)
    fixed = lambda i: (0, 0)
    tab = lambda i: (i % n_seq, 0)
    return pl.pallas_call(
        _inproj_kernel,
        grid=(t // tm,),
        in_specs=[
            pl.BlockSpec((tm, D_MODEL), row),
            pl.BlockSpec((1, D_MODEL), fixed),
            pl.BlockSpec((D_MODEL, D_IN), fixed),
            pl.BlockSpec((1, LANES), fixed),
            pl.BlockSpec((1, LANES), fixed),
            pl.BlockSpec((tm, LANES), tab),
            pl.BlockSpec((tm, LANES), tab),
        ],
        out_specs=[
            pl.BlockSpec((FNET_WIDTH // LANES, tm, LANES), lambda i: (0, i, 0)),
            pl.BlockSpec((tm, 2 * SGU_WIDTH), row),
            pl.BlockSpec((tm, Q_WIDTH), row),
            pl.BlockSpec((tm, KV_WIDTH), row),
            pl.BlockSpec((1, N_KV_HEADS, VT_ROWS, tm), lambda i: (i // n_seq, 0, 0, i % n_seq)),
        ],
        out_shape=[
            jax.ShapeDtypeStruct((FNET_WIDTH // LANES, t, LANES), F32),
            jax.ShapeDtypeStruct((t, 2 * SGU_WIDTH), F32),
            jax.ShapeDtypeStruct((t, Q_WIDTH), BF16),
            jax.ShapeDtypeStruct((t, KV_WIDTH), BF16),
            jax.ShapeDtypeStruct((t // seq, N_KV_HEADS, VT_ROWS, seq), BF16),
        ],
        compiler_params=_params("parallel"),
        name="inproj",
    )(x2, g, w_bf, gq2, gk2, cos_t, sin_t)


def _fourier_tables():
    n = np.arange(GRID_W)
    ang64 = 2.0 * np.pi * np.outer(n, n) / GRID_W
    c64, s64 = np.cos(ang64), np.sin(ang64)
    third = 1.0 / 8.0
    groups = FNET_WIDTH // HEAD_DIM
    eye = np.eye(groups)
    cs = np.concatenate([np.kron(eye, c64), -np.kron(eye, s64)], axis=1) * third
    f1 = np.block([[c64, s64], [-s64, c64]]) * third
    k = n[:, None] + GRID_W * n[None, :]
    ang = 2.0 * np.pi * k[:, :, None] * n[None, None, :] / (GRID_W * GRID_W)
    g2 = np.concatenate([np.cos(ang), np.sin(ang)], axis=2) * third
    return tuple(jnp.asarray(a, F32) for a in (cs, f1, g2))


FOURIER_UNROLL = 8


def _fourier_kernel(z_ref, cs_ref, f1_ref, g2_ref, y_ref, mid_ref):
    halves = range(FNET_WIDTH // LANES)

    def stage_a(grp, carry):
        b0 = grp * FOURIER_UNROLL
        xs = []
        for d in range(FOURIER_UNROLL):
            rows = pl.ds(b0 + d, GRID_W, stride=GRID_W)
            xs.append(jnp.concatenate([z_ref[j, 0, rows, :] for j in halves], axis=-1))
        xb = jnp.concatenate(xs, axis=0).astype(BF16)
        ri = jnp.dot(xb, cs_ref[...].astype(BF16), preferred_element_type=F32)
        st = jnp.concatenate(
            [jnp.concatenate([ri[d * GRID_W:(d + 1) * GRID_W, :FNET_WIDTH],
                              ri[d * GRID_W:(d + 1) * GRID_W, FNET_WIDTH:]], axis=0)
             for d in range(FOURIER_UNROLL)], axis=1).astype(BF16)
        t = jnp.dot(f1_ref[...].astype(BF16), st, preferred_element_type=F32)
        for d in range(FOURIER_UNROLL):
            dst = pl.ds(pl.multiple_of((b0 + d) * 2 * GRID_W, 2 * GRID_W), 2 * GRID_W)
            for j in halves:
                lanes = slice(d * FNET_WIDTH + j * LANES, d * FNET_WIDTH + (j + 1) * LANES)
                mid_ref[j, dst, :] = t[:, lanes]
        return carry

    lax.fori_loop(0, GRID_W // FOURIER_UNROLL, stage_a, 0)

    def stage_b(k1, carry):
        re_rows = pl.ds(k1, GRID_W, stride=2 * GRID_W)
        im_rows = pl.ds(k1 + GRID_W, GRID_W, stride=2 * GRID_W)
        tr = jnp.concatenate([mid_ref[j, re_rows, :] for j in halves], axis=-1)
        ti = jnp.concatenate([mid_ref[j, im_rows, :] for j in halves], axis=-1)
        st = jnp.concatenate([tr, ti], axis=0).astype(BF16)
        out = jnp.dot(g2_ref[k1].astype(BF16), st, preferred_element_type=F32)
        for j in halves:
            y_ref[j, 0, pl.ds(k1, GRID_W, stride=GRID_W), :] = out[:, j * LANES:(j + 1) * LANES]
        return carry

    lax.fori_loop(0, GRID_W, stage_b, 0, unroll=FOURIER_UNROLL)


def _fourier(zf4, tables):
    nh, b, s, w = zf4.shape
    cs, f1, g2 = tables
    return pl.pallas_call(
        _fourier_kernel,
        grid=(b,),
        in_specs=[
            pl.BlockSpec((nh, 1, s, w), lambda i: (0, i, 0, 0)),
            pl.BlockSpec(cs.shape, lambda i: (0, 0)),
            pl.BlockSpec(f1.shape, lambda i: (0, 0)),
            pl.BlockSpec(g2.shape, lambda i: (0, 0, 0)),
        ],
        out_specs=pl.BlockSpec((nh, 1, s, w), lambda i: (0, i, 0, 0)),
        out_shape=jax.ShapeDtypeStruct((nh, b, s, w), F32),
        scratch_shapes=[pltpu.VMEM((nh, 2 * s, w), F32)],
        compiler_params=_params("parallel"),
        name="fourier",
    )(zf4, cs, f1, g2)


def _gelu_tanh(x):
    c = math.sqrt(2.0 / math.pi)
    return 0.5 * x * (1.0 + jnp.tanh(c * (x + 0.044715 * (x * x * x))))


def _sgu_kernel(z_ref, w_ref, bias_ref, g_ref, y_ref):
    z = _gelu_tanh(z_ref[...])
    u = z[:, :SGU_WIDTH]
    v = z[:, SGU_WIDTH:]
    mu = jnp.mean(v, axis=-1, keepdims=True)
    d = v - mu
    var = jnp.mean(d * d, axis=-1, keepdims=True)
    vn = (d * lax.rsqrt(var + EPS) * g_ref[...]).astype(BF16)
    head = lax.broadcasted_iota(jnp.int32, (1, SGU_WIDTH), 1) // HEAD_DIM
    bias = bias_ref[...]
    for c in range(z.shape[0] // SGU_CHUNK):
        rows = slice(c * SGU_CHUNK, (c + 1) * SGU_CHUNK)
        vc = vn[rows]
        sv = jnp.dot(w_ref[0], vc, preferred_element_type=F32)
        for hd in range(1, SGU_HEADS):
            sv = jnp.where(head == hd, jnp.dot(w_ref[hd], vc, preferred_element_type=F32), sv)
        y_ref[rows, :] = u[rows] * (sv + bias)


def _sgu(zs2, w_bf, bias_full, g, tm=512):
    t = zs2.shape[0]
    return pl.pallas_call(
        _sgu_kernel,
        grid=(t // tm,),
        in_specs=[
            pl.BlockSpec((tm, 2 * SGU_WIDTH), lambda i: (i, 0)),
            pl.BlockSpec(w_bf.shape, lambda i: (0, 0, 0)),
            pl.BlockSpec(bias_full.shape, lambda i: (0, 0)),
            pl.BlockSpec((1, SGU_WIDTH), lambda i: (0, 0)),
        ],
        out_specs=pl.BlockSpec((tm, SGU_WIDTH), lambda i: (i, 0)),
        out_shape=jax.ShapeDtypeStruct((t, SGU_WIDTH), F32),
        compiler_params=_params("parallel"),
        name="sgu",
    )(zs2, w_bf, bias_full, g)


ATTN_UNIT = 256
ATTN_CHUNK = 64
ATTN_PIECES = 8


def _attn_kernel(q_ref, k_ref, vt_ref, o_ref, qpad_ref, s0_ref, s1_ref, p0_ref, p1_ref,
                 m0_ref, m1_ref, ot_ref):
    s_refs, p_refs, m_refs = (s0_ref, s1_ref), (p0_ref, p1_ref), (m0_ref, m1_ref)
    tq = q_ref.shape[1]
    seq = k_ref.shape[1]
    units_per_head = tq // ATTN_UNIT
    n_units = N_Q_HEADS * units_per_head
    units_per_kv = n_units // N_KV_HEADS
    lane = lax.broadcasted_iota(jnp.int32, (1, LANES), 1)

    for hd in range(N_Q_HEADS):
        kv = hd // Q_PER_KV
        qs = q_ref[0, :, (hd // 2) * LANES:(hd // 2 + 1) * LANES].astype(F32)
        if hd % 2 != kv:
            qs = pltpu.roll(qs, HEAD_DIM, 1)
        qpad_ref[hd * tq:(hd + 1) * tq, :] = jnp.where((lane // HEAD_DIM) == kv, qs, 0.0).astype(BF16)

    def unit_rows(u):
        return slice(u * ATTN_UNIT, (u + 1) * ATTN_UNIT)

    piece = seq // ATTN_PIECES

    def stage_a(u, slot):
        q_u = qpad_ref[unit_rows(u), :]
        acc = []

        def step(i):
            rows = slice(i * piece, (i + 1) * piece)
            s = lax.dot_general(k_ref[0, rows, :], q_u, (((1,), (1,)), ((), ())),
                                preferred_element_type=F32)
            s_refs[slot][rows, :] = s
            parts = [s[r:r + SUBLANES] for r in range(0, piece, SUBLANES)]
            while len(parts) > 1:
                parts = [jnp.maximum(parts[j], parts[j + 1]) for j in range(0, len(parts), 2)]
            acc.append(parts[0])

        def finish():
            m = functools.reduce(jnp.maximum, acc)
            m_refs[slot][...] = jnp.broadcast_to(jnp.max(m, axis=0, keepdims=True), m.shape)

        return [functools.partial(step, i) for i in range(ATTN_PIECES)], finish

    def stage_b(slot):
        s_ref, p_ref = s_refs[slot], p_refs[slot]

        def step(i):
            mb = jnp.broadcast_to(m_refs[slot][0:1, :], (ATTN_CHUNK, ATTN_UNIT))
            for c in range(i * piece // ATTN_CHUNK, (i + 1) * piece // ATTN_CHUNK):
                rows = slice(c * ATTN_CHUNK, (c + 1) * ATTN_CHUNK)
                p_ref[rows, :] = jnp.exp2(s_ref[rows, :] - mb).astype(BF16)

        return [functools.partial(step, i) for i in range(ATTN_PIECES)], None

    def stage_c(u, slot):
        acc = []

        def step(i):
            keys = slice(i * piece, (i + 1) * piece)
            acc.append(jnp.dot(vt_ref[0, u // units_per_kv, :, keys], p_refs[slot][keys, :],
                               preferred_element_type=F32))

        def finish():
            o = functools.reduce(jnp.add, acc)
            ot_ref[u] = o[:HEAD_DIM] / o[HEAD_DIM:HEAD_DIM + 1]

        return [functools.partial(step, i) for i in range(ATTN_PIECES)], finish

    def issue(*stages):
        for steps in zip(*[st[0] for st in stages]):
            for step in steps:
                step()
        for st in stages:
            if st[1] is not None:
                st[1]()

    issue(stage_a(0, 0))
    issue(stage_a(1, 1), stage_b(0))
    for t in range(2, n_units):
        issue(stage_a(t, t % 2), stage_b((t - 1) % 2), stage_c(t - 2, t % 2))
    issue(stage_b((n_units - 1) % 2), stage_c(n_units - 2, n_units % 2))
    issue(stage_c(n_units - 1, (n_units - 1) % 2))

    for j in range(Q_WIDTH // LANES):
        for r in range(units_per_head):
            pair = jnp.concatenate([ot_ref[(2 * j) * units_per_head + r],
                                    ot_ref[(2 * j + 1) * units_per_head + r]], axis=0)
            o_ref[0, r * ATTN_UNIT:(r + 1) * ATTN_UNIT, j * LANES:(j + 1) * LANES] = pair.T


def _attention(q3, k3, vt4, tq=512):
    b, s, _ = q3.shape
    n_units = N_Q_HEADS * tq // ATTN_UNIT
    return pl.pallas_call(
        _attn_kernel,
        grid=(b, s // tq),
        in_specs=[
            pl.BlockSpec((1, tq, Q_WIDTH), lambda i, j: (i, j, 0)),
            pl.BlockSpec((1, s, KV_WIDTH), lambda i, j: (i, 0, 0)),
            pl.BlockSpec((1, N_KV_HEADS, VT_ROWS, s), lambda i, j: (i, 0, 0, 0)),
        ],
        out_specs=pl.BlockSpec((1, tq, Q_WIDTH), lambda i, j: (i, j, 0)),
        out_shape=jax.ShapeDtypeStruct((b, s, Q_WIDTH), F32),
        scratch_shapes=[
            pltpu.VMEM((N_Q_HEADS * tq, LANES), BF16),
            pltpu.VMEM((s, ATTN_UNIT), F32),
            pltpu.VMEM((s, ATTN_UNIT), F32),
            pltpu.VMEM((s, ATTN_UNIT), BF16),
            pltpu.VMEM((s, ATTN_UNIT), BF16),
            pltpu.VMEM((SUBLANES, ATTN_UNIT), F32),
            pltpu.VMEM((SUBLANES, ATTN_UNIT), F32),
            pltpu.VMEM((n_units, HEAD_DIM, ATTN_UNIT), F32),
        ],
        compiler_params=_params("parallel", "parallel"),
        name="attention",
    )(q3, k3, vt4)


def _outproj_kernel(yf_ref, ys_ref, ya_ref, x_ref, gm_ref, w_ref, gp_ref, o_ref):
    a1 = FNET_WIDTH + SGU_WIDTH
    gm = gm_ref[...]
    for rs in range(x_ref.shape[0] // ROW_SPLIT):
        rows = slice(rs * ROW_SPLIT, (rs + 1) * ROW_SPLIT)
        yf = jnp.concatenate([yf_ref[j, rows, :] for j in range(FNET_WIDTH // LANES)], axis=-1)
        y = jnp.concatenate([
            _rms(yf, gm[:, :FNET_WIDTH]),
            _rms(ys_ref[rows, :], gm[:, FNET_WIDTH:a1]),
            _rms(ya_ref[rows, :], gm[:, a1:]),
        ], axis=-1).astype(BF16)
        m = jnp.dot(y, w_ref[...], preferred_element_type=F32)
        o_ref[rows, :] = x_ref[rows, :] + _rms(m, gp_ref[...])


def _outproj(yf2, ys2, ya2, x2, gm, w_bf, gp, tm=1024):
    t = x2.shape[0]
    row = lambda i: (i, 0)
    fixed = lambda i: (0, 0)
    return pl.pallas_call(
        _outproj_kernel,
        grid=(t // tm,),
        in_specs=[
            pl.BlockSpec((FNET_WIDTH // LANES, tm, LANES), lambda i: (0, i, 0)),
            pl.BlockSpec((tm, SGU_WIDTH), row),
            pl.BlockSpec((tm, Q_WIDTH), row),
            pl.BlockSpec((tm, D_MODEL), row),
            pl.BlockSpec((1, D_MIX), fixed),
            pl.BlockSpec((D_MIX, D_MODEL), fixed),
            pl.BlockSpec((1, D_MODEL), fixed),
        ],
        out_specs=pl.BlockSpec((tm, D_MODEL), row),
        out_shape=jax.ShapeDtypeStruct((t, D_MODEL), F32),
        compiler_params=_params("parallel"),
        name="outproj",
    )(yf2, ys2, ya2, x2, gm, w_bf, gp)


FFN_CHUNK = 256


def _ffn_kernel(x_ref, gpre_ref, wg_ref, wu_ref, wd_ref, gpost_ref, o_ref):
    x = x_ref[...]
    h = _rms(x, gpre_ref[...]).astype(BF16)
    acc = None
    for c in range(wg_ref.shape[1] // FFN_CHUNK):
        cols = slice(c * FFN_CHUNK, (c + 1) * FFN_CHUNK)
        gate = jnp.dot(h, wg_ref[:, cols], preferred_element_type=F32)
        up = jnp.dot(h, wu_ref[:, cols], preferred_element_type=F32)
        act = (gate * (1.0 / (1.0 + jnp.exp(-gate))) * up).astype(BF16)
        part = jnp.dot(act, wd_ref[cols, :], preferred_element_type=F32)
        acc = part if acc is None else acc + part
    o_ref[...] = x + _rms(acc, gpost_ref[...])


def _ffn(x2, gpre, wg_bf, wu_bf, wd_bf, gpost, tm=512):
    t = x2.shape[0]
    d_ff = wg_bf.shape[1]
    resident = pl.Buffered(1)
    return pl.pallas_call(
        _ffn_kernel,
        grid=(t // tm,),
        in_specs=[
            pl.BlockSpec((tm, D_MODEL), lambda i: (i, 0)),
            pl.BlockSpec((1, D_MODEL), lambda i: (0, 0)),
            pl.BlockSpec((D_MODEL, d_ff), lambda i: (0, 0), pipeline_mode=resident),
            pl.BlockSpec((D_MODEL, d_ff), lambda i: (0, 0), pipeline_mode=resident),
            pl.BlockSpec((d_ff, D_MODEL), lambda i: (0, 0), pipeline_mode=resident),
            pl.BlockSpec((1, D_MODEL), lambda i: (0, 0)),
        ],
        out_specs=pl.BlockSpec((tm, D_MODEL), lambda i: (i, 0)),
        out_shape=jax.ShapeDtypeStruct((t, D_MODEL), F32),
        compiler_params=_params("parallel"),
        name="ffn",
    )(x2, gpre, wg_bf, wu_bf, wd_bf, gpost)


def _rope_tables(seq):
    pos = jnp.arange(seq)
    row = (pos // GRID_W).astype(F32)
    col = (pos % GRID_W).astype(F32)
    freqs = ROPE_THETA ** (-jnp.arange(ROPE_FREQS, dtype=F32) / ROPE_FREQS)
    ang_r = row[:, None] * freqs
    ang_c = col[:, None] * freqs
    cos_r, sin_r, cos_c, sin_c = jnp.cos(ang_r), jnp.sin(ang_r), jnp.cos(ang_c), jnp.sin(ang_c)
    cos_h = jnp.concatenate([cos_r, cos_r, cos_c, cos_c], axis=-1)
    sin_h = jnp.concatenate([-sin_r, sin_r, -sin_c, sin_c], axis=-1)
    reps = LANES // HEAD_DIM
    return jnp.tile(cos_h, (1, reps)), jnp.tile(sin_h, (1, reps))


@jax.jit
def kernel(x, g_pre_mix, w_in, sgu_w, sgu_b, sgu_g, g_q, g_k, g_mix, w_out,
           g_post_mix, g_pre_ffn, w_gate, w_up, w_down, g_post_ffn):
    b, s, d = x.shape
    depth = w_in.shape[0]
    t = b * s
    cos_t, sin_t = _rope_tables(s)
    ftabs = _fourier_tables()
    reps = LANES // HEAD_DIM
    x2 = x.reshape(t, d)
    for l in range(depth):
        zf, zs, q, k, v = _inproj(
            x2, g_pre_mix[l][None], w_in[l].astype(BF16),
            jnp.tile(g_q[l], reps)[None], jnp.tile(g_k[l], reps)[None], cos_t, sin_t, s)
        nh = FNET_WIDTH // LANES
        yf = _fourier(zf.reshape(nh, b, s, LANES), ftabs).reshape(nh, t, LANES)
        bias_full = jnp.repeat(sgu_b[l].T, HEAD_DIM, axis=1)
        ys = _sgu(zs, sgu_w[l].astype(BF16), bias_full, sgu_g[l][None])
        ya = _attention(q.reshape(b, s, Q_WIDTH), k.reshape(b, s, KV_WIDTH),
                        v).reshape(t, Q_WIDTH)
        x2 = _outproj(yf, ys, ya, x2, g_mix[l][None], w_out[l].astype(BF16), g_post_mix[l][None])
        x2 = _ffn(x2, g_pre_ffn[l][None], w_gate[l].astype(BF16), w_up[l].astype(BF16),
                  w_down[l].astype(BF16), g_post_ffn[l][None])
    return x2.reshape(b, s, d)
```

```python
import functools
import math

import numpy as np
import jax
import jax.numpy as jnp
from jax import lax
from jax.experimental import pallas as pl
from jax.experimental.pallas import tpu as pltpu

F32 = jnp.float32
BF16 = jnp.bfloat16

D_MODEL = 1024
GRID_W = 64
HEAD_DIM = 64
FNET_WIDTH = 256
SGU_WIDTH = 256
SGU_HEADS = 4
SGU_CHUNK = 128
N_Q_HEADS = 8
N_KV_HEADS = 2
Q_PER_KV = N_Q_HEADS // N_KV_HEADS
Q_WIDTH = N_Q_HEADS * HEAD_DIM
KV_WIDTH = N_KV_HEADS * HEAD_DIM
ROPE_THETA = 10000.0
ROPE_FREQS = HEAD_DIM // 4
D_MIX = FNET_WIDTH + SGU_WIDTH + Q_WIDTH
D_IN = FNET_WIDTH + 2 * SGU_WIDTH + Q_WIDTH + 2 * KV_WIDTH
EPS = 1e-6

LANES = 128
SUBLANES = 8
BF16_SUBLANES = 16
VT_ROWS = HEAD_DIM + BF16_SUBLANES
VMEM_LIMIT = 56 * 1024 * 1024

O_SGU = FNET_WIDTH
O_Q = O_SGU + 2 * SGU_WIDTH
O_K = O_Q + Q_WIDTH
O_V = O_K + KV_WIDTH


def _params(*sem):
    return pltpu.CompilerParams(dimension_semantics=sem, vmem_limit_bytes=VMEM_LIMIT)


ROW_SPLIT = 256


def _rms(x, g):
    return x * lax.rsqrt(jnp.mean(x * x, axis=-1, keepdims=True) + EPS) * g


def _head_norm_rope(xs, g, cos, sin, seg_ones, swap_lo):
    ms = jnp.dot((xs * xs).astype(BF16), seg_ones, preferred_element_type=F32) * (1.0 / HEAD_DIM)
    xn = xs * lax.rsqrt(ms + EPS) * g
    sw = jnp.where(swap_lo, pltpu.roll(xn, LANES - ROPE_FREQS, 1), pltpu.roll(xn, ROPE_FREQS, 1))
    return xn * cos + sw * sin


def _gelu_tanh(x):
    c = math.sqrt(2.0 / math.pi)
    return 0.5 * x * (1.0 + jnp.tanh(c * (x + 0.044715 * (x * x * x))))


def _spatial_gate(zs, w_ref, bias, g):
    z = _gelu_tanh(zs)
    u = z[:, :SGU_WIDTH]
    v = z[:, SGU_WIDTH:]
    d = v - jnp.mean(v, axis=-1, keepdims=True)
    var = jnp.mean(d * d, axis=-1, keepdims=True)
    vn = (d * lax.rsqrt(var + EPS) * g).astype(BF16)
    head = lax.broadcasted_iota(jnp.int32, (1, SGU_WIDTH), 1) // HEAD_DIM
    out = []
    for c in range(zs.shape[0] // SGU_CHUNK):
        rows = slice(c * SGU_CHUNK, (c + 1) * SGU_CHUNK)
        vc = vn[rows]
        sv = jnp.dot(w_ref[0], vc, preferred_element_type=F32)
        for hd in range(1, SGU_HEADS):
            sv = jnp.where(head == hd, jnp.dot(w_ref[hd], vc, preferred_element_type=F32), sv)
        out.append(u[rows] * (sv + bias))
    return jnp.concatenate(out, axis=0)


def _inproj_kernel(x_ref, g_ref, w_ref, gq_ref, gk_ref, cos_ref, sin_ref, sw_ref, sb_ref, sg_ref,
                   zf_ref, ys_ref, q_ref, k_ref, vt_ref):
    r = lax.broadcasted_iota(jnp.int32, (LANES, LANES), 0) // HEAD_DIM
    c = lax.broadcasted_iota(jnp.int32, (LANES, LANES), 1) // HEAD_DIM
    seg_ones = jnp.where(r == c, 1.0, 0.0).astype(BF16)
    lane = lax.broadcasted_iota(jnp.int32, (1, LANES), 1)
    swap_lo = (lane % (2 * ROPE_FREQS)) < ROPE_FREQS
    scale = HEAD_DIM ** -0.5 * math.log2(math.e)
    ones = jnp.ones((VT_ROWS - HEAD_DIM, ROW_SPLIT), BF16)
    for rs in range(x_ref.shape[0] // ROW_SPLIT):
        rows = slice(rs * ROW_SPLIT, (rs + 1) * ROW_SPLIT)
        h = _rms(x_ref[rows, :], g_ref[...]).astype(BF16)
        z = jnp.dot(h, w_ref[...], preferred_element_type=F32)
        for j in range(FNET_WIDTH // LANES):
            zf_ref[j, rows, :] = z[:, j * LANES:(j + 1) * LANES]
        ys_ref[rows, :] = _spatial_gate(z[:, O_SGU:O_Q], sw_ref, sb_ref[...], sg_ref[...])
        cos = cos_ref[rows, :]
        sin = sin_ref[rows, :]
        for j in range(Q_WIDTH // LANES):
            qs = z[:, O_Q + j * LANES:O_Q + (j + 1) * LANES]
            qr = _head_norm_rope(qs, gq_ref[...], cos, sin, seg_ones, swap_lo)
            q_ref[rows, j * LANES:(j + 1) * LANES] = (qr * scale).astype(BF16)
        k_ref[rows, :] = _head_norm_rope(z[:, O_K:O_V], gk_ref[...], cos, sin, seg_ones,
                                         swap_lo).astype(BF16)
        vt = z[:, O_V:].T
        for kv in range(N_KV_HEADS):
            vt_ref[0, kv, :HEAD_DIM, rows] = vt[kv * HEAD_DIM:(kv + 1) * HEAD_DIM].astype(BF16)
            vt_ref[0, kv, HEAD_DIM:, rows] = ones


def _inproj(x2, g, w_bf, gq2, gk2, cos_t, sin_t, sgu_w_bf, sgu_bias, sgu_g, seq, tm=1024):
    t = x2.shape[0]
    n_seq = seq // tm
    row = lambda i: (i, 0)
    fixed = lambda i: (0, 0)
    tab = lambda i: (i % n_seq, 0)
    return pl.pallas_call(
        _inproj_kernel,
        grid=(t // tm,),
        in_specs=[
            pl.BlockSpec((tm, D_MODEL), row),
            pl.BlockSpec((1, D_MODEL), fixed),
            pl.BlockSpec((D_MODEL, D_IN), fixed),
            pl.BlockSpec((1, LANES), fixed),
            pl.BlockSpec((1, LANES), fixed),
            pl.BlockSpec((tm, LANES), tab),
            pl.BlockSpec((tm, LANES), tab),
            pl.BlockSpec(sgu_w_bf.shape, lambda i: (0, 0, 0)),
            pl.BlockSpec(sgu_bias.shape, fixed),
            pl.BlockSpec((1, SGU_WIDTH), fixed),
        ],
        out_specs=[
            pl.BlockSpec((FNET_WIDTH // LANES, tm, LANES), lambda i: (0, i, 0)),
            pl.BlockSpec((tm, SGU_WIDTH), row),
            pl.BlockSpec((tm, Q_WIDTH), row),
            pl.BlockSpec((tm, KV_WIDTH), row),
            pl.BlockSpec((1, N_KV_HEADS, VT_ROWS, tm), lambda i: (i // n_seq, 0, 0, i % n_seq)),
        ],
        out_shape=[
            jax.ShapeDtypeStruct((FNET_WIDTH // LANES, t, LANES), F32),
            jax.ShapeDtypeStruct((t, SGU_WIDTH), F32),
            jax.ShapeDtypeStruct((t, Q_WIDTH), BF16),
            jax.ShapeDtypeStruct((t, KV_WIDTH), BF16),
            jax.ShapeDtypeStruct((t // seq, N_KV_HEADS, VT_ROWS, seq), BF16),
        ],
        compiler_params=_params("parallel"),
        name="inproj",
    )(x2, g, w_bf, gq2, gk2, cos_t, sin_t, sgu_w_bf, sgu_bias, sgu_g)


def _fourier_tables():
    n = np.arange(GRID_W)
    ang64 = 2.0 * np.pi * np.outer(n, n) / GRID_W
    c64, s64 = np.cos(ang64), np.sin(ang64)
    third = 1.0 / 8.0
    groups = FNET_WIDTH // HEAD_DIM
    eye = np.eye(groups)
    cs = np.concatenate([np.kron(eye, c64), -np.kron(eye, s64)], axis=1) * third
    f1 = np.block([[c64, s64], [-s64, c64]]) * third
    k = n[:, None] + GRID_W * n[None, :]
    ang = 2.0 * np.pi * k[:, :, None] * n[None, None, :] / (GRID_W * GRID_W)
    g2 = np.concatenate([np.cos(ang), np.sin(ang)], axis=2) * third
    return tuple(jnp.asarray(a, F32) for a in (cs, f1, g2))


FOURIER_UNROLL = 8


def _fourier_kernel(z_ref, cs_ref, f1_ref, g2_ref, y_ref, mid_ref):
    halves = range(FNET_WIDTH // LANES)

    def stage_a(grp, carry):
        b0 = grp * FOURIER_UNROLL
        xs = []
        for d in range(FOURIER_UNROLL):
            rows = pl.ds(b0 + d, GRID_W, stride=GRID_W)
            xs.append(jnp.concatenate([z_ref[j, 0, rows, :] for j in halves], axis=-1))
        xb = jnp.concatenate(xs, axis=0).astype(BF16)
        ri = jnp.dot(xb, cs_ref[...].astype(BF16), preferred_element_type=F32)
        st = jnp.concatenate(
            [jnp.concatenate([ri[d * GRID_W:(d + 1) * GRID_W, :FNET_WIDTH],
                              ri[d * GRID_W:(d + 1) * GRID_W, FNET_WIDTH:]], axis=0)
             for d in range(FOURIER_UNROLL)], axis=1).astype(BF16)
        t = jnp.dot(f1_ref[...].astype(BF16), st, preferred_element_type=F32)
        for d in range(FOURIER_UNROLL):
            dst = pl.ds(pl.multiple_of((b0 + d) * 2 * GRID_W, 2 * GRID_W), 2 * GRID_W)
            for j in halves:
                lanes = slice(d * FNET_WIDTH + j * LANES, d * FNET_WIDTH + (j + 1) * LANES)
                mid_ref[j, dst, :] = t[:, lanes]
        return carry

    lax.fori_loop(0, GRID_W // FOURIER_UNROLL, stage_a, 0)

    def stage_b(k1, carry):
        re_rows = pl.ds(k1, GRID_W, stride=2 * GRID_W)
        im_rows = pl.ds(k1 + GRID_W, GRID_W, stride=2 * GRID_W)
        tr = jnp.concatenate([mid_ref[j, re_rows, :] for j in halves], axis=-1)
        ti = jnp.concatenate([mid_ref[j, im_rows, :] for j in halves], axis=-1)
        st = jnp.concatenate([tr, ti], axis=0).astype(BF16)
        out = jnp.dot(g2_ref[k1].astype(BF16), st, preferred_element_type=F32)
        for j in halves:
            y_ref[j, 0, pl.ds(k1, GRID_W, stride=GRID_W), :] = out[:, j * LANES:(j + 1) * LANES]
        return carry

    lax.fori_loop(0, GRID_W, stage_b, 0, unroll=FOURIER_UNROLL)


def _fourier(zf4, tables):
    nh, b, s, w = zf4.shape
    cs, f1, g2 = tables
    return pl.pallas_call(
        _fourier_kernel,
        grid=(b,),
        in_specs=[
            pl.BlockSpec((nh, 1, s, w), lambda i: (0, i, 0, 0)),
            pl.BlockSpec(cs.shape, lambda i: (0, 0)),
            pl.BlockSpec(f1.shape, lambda i: (0, 0)),
            pl.BlockSpec(g2.shape, lambda i: (0, 0, 0)),
        ],
        out_specs=pl.BlockSpec((nh, 1, s, w), lambda i: (0, i, 0, 0)),
        out_shape=jax.ShapeDtypeStruct((nh, b, s, w), F32),
        scratch_shapes=[pltpu.VMEM((nh, 2 * s, w), F32)],
        compiler_params=_params("parallel"),
        name="fourier",
    )(zf4, cs, f1, g2)


ATTN_UNIT = 256
ATTN_CHUNK = 64
ATTN_PIECES = 16


def _attn_kernel(q_ref, k_ref, vt_ref, o_ref, qpad_ref, s0_ref, s1_ref, p0_ref, p1_ref,
                 m0_ref, m1_ref, ot_ref):
    s_refs, p_refs, m_refs = (s0_ref, s1_ref), (p0_ref, p1_ref), (m0_ref, m1_ref)
    tq = q_ref.shape[1]
    seq = k_ref.shape[1]
    units_per_head = tq // ATTN_UNIT
    n_units = N_Q_HEADS * units_per_head
    units_per_kv = n_units // N_KV_HEADS
    lane = lax.broadcasted_iota(jnp.int32, (1, LANES), 1)

    for hd in range(N_Q_HEADS):
        kv = hd // Q_PER_KV
        qs = q_ref[0, :, (hd // 2) * LANES:(hd // 2 + 1) * LANES].astype(F32)
        if hd % 2 != kv:
            qs = pltpu.roll(qs, HEAD_DIM, 1)
        qpad_ref[hd * tq:(hd + 1) * tq, :] = jnp.where((lane // HEAD_DIM) == kv, qs, 0.0).astype(BF16)

    def unit_rows(u):
        return slice(u * ATTN_UNIT, (u + 1) * ATTN_UNIT)

    piece = seq // ATTN_PIECES

    def stage_a(u, slot):
        q_u = qpad_ref[unit_rows(u), :]
        acc = []

        def step(i):
            rows = slice(i * piece, (i + 1) * piece)
            s = lax.dot_general(k_ref[0, rows, :], q_u, (((1,), (1,)), ((), ())),
                                preferred_element_type=F32)
            s_refs[slot][rows, :] = s
            parts = [s[r:r + SUBLANES] for r in range(0, piece, SUBLANES)]
            while len(parts) > 1:
                parts = [jnp.maximum(parts[j], parts[j + 1]) for j in range(0, len(parts), 2)]
            acc.append(parts[0])

        def finish():
            m = functools.reduce(jnp.maximum, acc)
            m_refs[slot][...] = jnp.broadcast_to(jnp.max(m, axis=0, keepdims=True), m.shape)

        return [functools.partial(step, i) for i in range(ATTN_PIECES)], finish

    def stage_b(slot):
        s_ref, p_ref = s_refs[slot], p_refs[slot]

        def step(i):
            mb = jnp.broadcast_to(m_refs[slot][0:1, :], (ATTN_CHUNK, ATTN_UNIT))
            for c in range(i * piece // ATTN_CHUNK, (i + 1) * piece // ATTN_CHUNK):
                rows = slice(c * ATTN_CHUNK, (c + 1) * ATTN_CHUNK)
                p_ref[rows, :] = jnp.exp2(s_ref[rows, :] - mb).astype(BF16)

        return [functools.partial(step, i) for i in range(ATTN_PIECES)], None

    def stage_c(u, slot):
        acc = []

        def step(i):
            keys = slice(i * piece, (i + 1) * piece)
            acc.append(jnp.dot(vt_ref[0, u // units_per_kv, :, keys], p_refs[slot][keys, :],
                               preferred_element_type=F32))

        def finish():
            o = functools.reduce(jnp.add, acc)
            ot_ref[u] = o[:HEAD_DIM] / o[HEAD_DIM:HEAD_DIM + 1]

        return [functools.partial(step, i) for i in range(ATTN_PIECES)], finish

    def issue(*stages):
        for steps in zip(*[st[0] for st in stages]):
            for step in steps:
                step()
        for st in stages:
            if st[1] is not None:
                st[1]()

    issue(stage_a(0, 0))
    issue(stage_a(1, 1), stage_b(0))
    for t in range(2, n_units):
        issue(stage_a(t, t % 2), stage_b((t - 1) % 2), stage_c(t - 2, t % 2))
    issue(stage_b((n_units - 1) % 2), stage_c(n_units - 2, n_units % 2))
    issue(stage_c(n_units - 1, (n_units - 1) % 2))

    for j in range(Q_WIDTH // LANES):
        for r in range(units_per_head):
            pair = jnp.concatenate([ot_ref[(2 * j) * units_per_head + r],
                                    ot_ref[(2 * j + 1) * units_per_head + r]], axis=0)
            o_ref[0, r * ATTN_UNIT:(r + 1) * ATTN_UNIT, j * LANES:(j + 1) * LANES] = pair.T


def _attention(q3, k3, vt4, tq=512):
    b, s, _ = q3.shape
    n_units = N_Q_HEADS * tq // ATTN_UNIT
    return pl.pallas_call(
        _attn_kernel,
        grid=(b, s // tq),
        in_specs=[
            pl.BlockSpec((1, tq, Q_WIDTH), lambda i, j: (i, j, 0)),
            pl.BlockSpec((1, s, KV_WIDTH), lambda i, j: (i, 0, 0)),
            pl.BlockSpec((1, N_KV_HEADS, VT_ROWS, s), lambda i, j: (i, 0, 0, 0)),
        ],
        out_specs=pl.BlockSpec((1, tq, Q_WIDTH), lambda i, j: (i, j, 0)),
        out_shape=jax.ShapeDtypeStruct((b, s, Q_WIDTH), F32),
        scratch_shapes=[
            pltpu.VMEM((N_Q_HEADS * tq, LANES), BF16),
            pltpu.VMEM((s, ATTN_UNIT), F32),
            pltpu.VMEM((s, ATTN_UNIT), F32),
            pltpu.VMEM((s, ATTN_UNIT), BF16),
            pltpu.VMEM((s, ATTN_UNIT), BF16),
            pltpu.VMEM((SUBLANES, ATTN_UNIT), F32),
            pltpu.VMEM((SUBLANES, ATTN_UNIT), F32),
            pltpu.VMEM((n_units, HEAD_DIM, ATTN_UNIT), F32),
        ],
        compiler_params=_params("parallel", "parallel"),
        name="attention",
    )(q3, k3, vt4)


FFN_CHUNK = 256


def _mix_ffn_kernel(yf_ref, ys_ref, ya_ref, x_ref, gm_ref, wo_ref, gpm_ref,
                    gpre_ref, wg_ref, wu_ref, wd_ref, gpost_ref, o_ref):
    a1 = FNET_WIDTH + SGU_WIDTH
    gm = gm_ref[...]
    x1 = []
    for rs in range(x_ref.shape[0] // ROW_SPLIT):
        rows = slice(rs * ROW_SPLIT, (rs + 1) * ROW_SPLIT)
        yf = jnp.concatenate([yf_ref[j, rows, :] for j in range(FNET_WIDTH // LANES)], axis=-1)
        y = jnp.concatenate([
            _rms(yf, gm[:, :FNET_WIDTH]),
            _rms(ys_ref[rows, :], gm[:, FNET_WIDTH:a1]),
            _rms(ya_ref[rows, :], gm[:, a1:]),
        ], axis=-1).astype(BF16)
        m = jnp.dot(y, wo_ref[...], preferred_element_type=F32)
        x1.append(x_ref[rows, :] + _rms(m, gpm_ref[...]))
    x1 = jnp.concatenate(x1, axis=0)
    h = _rms(x1, gpre_ref[...]).astype(BF16)
    acc = None
    for c in range(wg_ref.shape[1] // FFN_CHUNK):
        cols = slice(c * FFN_CHUNK, (c + 1) * FFN_CHUNK)
        gate = jnp.dot(h, wg_ref[:, cols], preferred_element_type=F32)
        up = jnp.dot(h, wu_ref[:, cols], preferred_element_type=F32)
        act = (gate * (1.0 / (1.0 + jnp.exp(-gate))) * up).astype(BF16)
        part = jnp.dot(act, wd_ref[cols, :], preferred_element_type=F32)
        acc = part if acc is None else acc + part
    o_ref[...] = x1 + _rms(acc, gpost_ref[...])


def _mix_ffn(yf2, ys2, ya2, x2, gm, wo_bf, gpm, gpre, wg_bf, wu_bf, wd_bf, gpost, tm=512):
    t = x2.shape[0]
    d_ff = wg_bf.shape[1]
    row = lambda i: (i, 0)
    fixed = lambda i: (0, 0)
    resident = pl.Buffered(1)
    return pl.pallas_call(
        _mix_ffn_kernel,
        grid=(t // tm,),
        in_specs=[
            pl.BlockSpec((FNET_WIDTH // LANES, tm, LANES), lambda i: (0, i, 0)),
            pl.BlockSpec((tm, SGU_WIDTH), row),
            pl.BlockSpec((tm, Q_WIDTH), row),
            pl.BlockSpec((tm, D_MODEL), row),
            pl.BlockSpec((1, D_MIX), fixed),
            pl.BlockSpec((D_MIX, D_MODEL), fixed, pipeline_mode=resident),
            pl.BlockSpec((1, D_MODEL), fixed),
            pl.BlockSpec((1, D_MODEL), fixed),
            pl.BlockSpec((D_MODEL, d_ff), fixed, pipeline_mode=resident),
            pl.BlockSpec((D_MODEL, d_ff), fixed, pipeline_mode=resident),
            pl.BlockSpec((d_ff, D_MODEL), fixed, pipeline_mode=resident),
            pl.BlockSpec((1, D_MODEL), fixed),
        ],
        out_specs=pl.BlockSpec((tm, D_MODEL), row),
        out_shape=jax.ShapeDtypeStruct((t, D_MODEL), F32),
        compiler_params=_params("parallel"),
        name="mix_ffn",
    )(yf2, ys2, ya2, x2, gm, wo_bf, gpm, gpre, wg_bf, wu_bf, wd_bf, gpost)


def _rope_tables(seq):
    pos = jnp.arange(seq)
    row = (pos // GRID_W).astype(F32)
    col = (pos % GRID_W).astype(F32)
    freqs = ROPE_THETA ** (-jnp.arange(ROPE_FREQS, dtype=F32) / ROPE_FREQS)
    ang_r = row[:, None] * freqs
    ang_c = col[:, None] * freqs
    cos_r, sin_r, cos_c, sin_c = jnp.cos(ang_r), jnp.sin(ang_r), jnp.cos(ang_c), jnp.sin(ang_c)
    cos_h = jnp.concatenate([cos_r, cos_r, cos_c, cos_c], axis=-1)
    sin_h = jnp.concatenate([-sin_r, sin_r, -sin_c, sin_c], axis=-1)
    reps = LANES // HEAD_DIM
    return jnp.tile(cos_h, (1, reps)), jnp.tile(sin_h, (1, reps))


@jax.jit
def kernel(x, g_pre_mix, w_in, sgu_w, sgu_b, sgu_g, g_q, g_k, g_mix, w_out,
           g_post_mix, g_pre_ffn, w_gate, w_up, w_down, g_post_ffn):
    b, s, d = x.shape
    depth = w_in.shape[0]
    t = b * s
    cos_t, sin_t = _rope_tables(s)
    ftabs = _fourier_tables()
    reps = LANES // HEAD_DIM
    x2 = x.reshape(t, d)
    for l in range(depth):
        bias_full = jnp.repeat(sgu_b[l].T, HEAD_DIM, axis=1)
        zf, ys, q, k, vt = _inproj(
            x2, g_pre_mix[l][None], w_in[l].astype(BF16),
            jnp.tile(g_q[l], reps)[None], jnp.tile(g_k[l], reps)[None], cos_t, sin_t,
            sgu_w[l].astype(BF16), bias_full, sgu_g[l][None], s)
        nh = FNET_WIDTH // LANES
        yf = _fourier(zf.reshape(nh, b, s, LANES), ftabs).reshape(nh, t, LANES)
        ya = _attention(q.reshape(b, s, Q_WIDTH), k.reshape(b, s, KV_WIDTH),
                        vt).reshape(t, Q_WIDTH)
        x2 = _mix_ffn(yf, ys, ya, x2, g_mix[l][None], w_out[l].astype(BF16), g_post_mix[l][None],
                      g_pre_ffn[l][None], w_gate[l].astype(BF16), w_up[l].astype(BF16),
                      w_down[l].astype(BF16), g_post_ffn[l][None])
    return x2.reshape(b, s, d)
```

```python
import functools
import math

import numpy as np
import jax
import jax.numpy as jnp
from jax import lax
from jax.experimental import pallas as pl
from jax.experimental.pallas import tpu as pltpu

F32 = jnp.float32
BF16 = jnp.bfloat16

D_MODEL = 1024
GRID_W = 64
HEAD_DIM = 64
FNET_WIDTH = 256
SGU_WIDTH = 256
SGU_HEADS = 4
SGU_CHUNK = 128
N_Q_HEADS = 8
N_KV_HEADS = 2
Q_PER_KV = N_Q_HEADS // N_KV_HEADS
Q_WIDTH = N_Q_HEADS * HEAD_DIM
KV_WIDTH = N_KV_HEADS * HEAD_DIM
ROPE_THETA = 10000.0
ROPE_FREQS = HEAD_DIM // 4
D_MIX = FNET_WIDTH + SGU_WIDTH + Q_WIDTH
D_IN = FNET_WIDTH + 2 * SGU_WIDTH + Q_WIDTH + 2 * KV_WIDTH
EPS = 1e-6

LANES = 128
SUBLANES = 8
BF16_SUBLANES = 16
VT_ROWS = HEAD_DIM + BF16_SUBLANES
VMEM_LIMIT = 56 * 1024 * 1024

O_SGU = FNET_WIDTH
O_Q = O_SGU + 2 * SGU_WIDTH
O_K = O_Q + Q_WIDTH
O_V = O_K + KV_WIDTH


def _params(*sem):
    return pltpu.CompilerParams(dimension_semantics=sem, vmem_limit_bytes=VMEM_LIMIT)


ROW_SPLIT = 256


def _rms(x, g):
    return x * lax.rsqrt(jnp.mean(x * x, axis=-1, keepdims=True) + EPS) * g


def _head_norm_rope(xs, g, cos, sin, seg_ones, swap_lo):
    ms = jnp.dot((xs * xs).astype(BF16), seg_ones, preferred_element_type=F32) * (1.0 / HEAD_DIM)
    xn = xs * lax.rsqrt(ms + EPS) * g
    sw = jnp.where(swap_lo, pltpu.roll(xn, LANES - ROPE_FREQS, 1), pltpu.roll(xn, ROPE_FREQS, 1))
    return xn * cos + sw * sin


def _gelu_tanh(x):
    c = math.sqrt(2.0 / math.pi)
    return 0.5 * x * (1.0 + jnp.tanh(c * (x + 0.044715 * (x * x * x))))


def _spatial_gate(zs, w_ref, bias, g):
    z = _gelu_tanh(zs)
    u = z[:, :SGU_WIDTH]
    v = z[:, SGU_WIDTH:]
    d = v - jnp.mean(v, axis=-1, keepdims=True)
    var = jnp.mean(d * d, axis=-1, keepdims=True)
    vn = (d * lax.rsqrt(var + EPS) * g).astype(BF16)
    head = lax.broadcasted_iota(jnp.int32, (1, SGU_WIDTH), 1) // HEAD_DIM
    out = []
    for c in range(zs.shape[0] // SGU_CHUNK):
        rows = slice(c * SGU_CHUNK, (c + 1) * SGU_CHUNK)
        vc = vn[rows]
        sv = jnp.dot(w_ref[0], vc, preferred_element_type=F32)
        for hd in range(1, SGU_HEADS):
            sv = jnp.where(head == hd, jnp.dot(w_ref[hd], vc, preferred_element_type=F32), sv)
        out.append(u[rows] * (sv + bias))
    return jnp.concatenate(out, axis=0)


def _inproj_kernel(x_ref, g_ref, w_ref, gq_ref, gk_ref, cos_ref, sin_ref, sw_ref, sb_ref, sg_ref,
                   zf_ref, ys_ref, q_ref, k_ref, vt_ref):
    r = lax.broadcasted_iota(jnp.int32, (LANES, LANES), 0) // HEAD_DIM
    c = lax.broadcasted_iota(jnp.int32, (LANES, LANES), 1) // HEAD_DIM
    seg_ones = jnp.where(r == c, 1.0, 0.0).astype(BF16)
    lane = lax.broadcasted_iota(jnp.int32, (1, LANES), 1)
    swap_lo = (lane % (2 * ROPE_FREQS)) < ROPE_FREQS
    scale = HEAD_DIM ** -0.5 * math.log2(math.e)
    ones = jnp.ones((VT_ROWS - HEAD_DIM, ROW_SPLIT), BF16)
    n_groups = x_ref.shape[0] // ROW_SPLIT

    def project(rs):
        rows = slice(rs * ROW_SPLIT, (rs + 1) * ROW_SPLIT)
        h = _rms(x_ref[rows, :], g_ref[...]).astype(BF16)
        return jnp.dot(h, w_ref[...], preferred_element_type=F32)

    z_next = project(0)
    for rs in range(n_groups):
        rows = slice(rs * ROW_SPLIT, (rs + 1) * ROW_SPLIT)
        z = z_next
        if rs + 1 < n_groups:
            z_next = project(rs + 1)
        for j in range(FNET_WIDTH // LANES):
            zf_ref[j, rows, :] = z[:, j * LANES:(j + 1) * LANES]
        ys_ref[rows, :] = _spatial_gate(z[:, O_SGU:O_Q], sw_ref, sb_ref[...], sg_ref[...])
        cos = cos_ref[rows, :]
        sin = sin_ref[rows, :]
        for j in range(Q_WIDTH // LANES):
            qs = z[:, O_Q + j * LANES:O_Q + (j + 1) * LANES]
            qr = _head_norm_rope(qs, gq_ref[...], cos, sin, seg_ones, swap_lo)
            q_ref[rows, j * LANES:(j + 1) * LANES] = (qr * scale).astype(BF16)
        k_ref[rows, :] = _head_norm_rope(z[:, O_K:O_V], gk_ref[...], cos, sin, seg_ones,
                                         swap_lo).astype(BF16)
        vt = z[:, O_V:].T
        for kv in range(N_KV_HEADS):
            vt_ref[0, kv, :HEAD_DIM, rows] = vt[kv * HEAD_DIM:(kv + 1) * HEAD_DIM].astype(BF16)
            vt_ref[0, kv, HEAD_DIM:, rows] = ones


def _inproj(x2, g, w_bf, gq2, gk2, cos_t, sin_t, sgu_w_bf, sgu_bias, sgu_g, seq, tm=1024):
    t = x2.shape[0]
    n_seq = seq // tm
    row = lambda i: (i, 0)
    fixed = lambda i: (0, 0)
    tab = lambda i: (i % n_seq, 0)
    return pl.pallas_call(
        _inproj_kernel,
        grid=(t // tm,),
        in_specs=[
            pl.BlockSpec((tm, D_MODEL), row),
            pl.BlockSpec((1, D_MODEL), fixed),
            pl.BlockSpec((D_MODEL, D_IN), fixed),
            pl.BlockSpec((1, LANES), fixed),
            pl.BlockSpec((1, LANES), fixed),
            pl.BlockSpec((tm, LANES), tab),
            pl.BlockSpec((tm, LANES), tab),
            pl.BlockSpec(sgu_w_bf.shape, lambda i: (0, 0, 0)),
            pl.BlockSpec(sgu_bias.shape, fixed),
            pl.BlockSpec((1, SGU_WIDTH), fixed),
        ],
        out_specs=[
            pl.BlockSpec((FNET_WIDTH // LANES, tm, LANES), lambda i: (0, i, 0)),
            pl.BlockSpec((tm, SGU_WIDTH), row),
            pl.BlockSpec((tm, Q_WIDTH), row),
            pl.BlockSpec((tm, KV_WIDTH), row),
            pl.BlockSpec((1, N_KV_HEADS, VT_ROWS, tm), lambda i: (i // n_seq, 0, 0, i % n_seq)),
        ],
        out_shape=[
            jax.ShapeDtypeStruct((FNET_WIDTH // LANES, t, LANES), F32),
            jax.ShapeDtypeStruct((t, SGU_WIDTH), F32),
            jax.ShapeDtypeStruct((t, Q_WIDTH), BF16),
            jax.ShapeDtypeStruct((t, KV_WIDTH), BF16),
            jax.ShapeDtypeStruct((t // seq, N_KV_HEADS, VT_ROWS, seq), BF16),
        ],
        compiler_params=_params("parallel"),
        name="inproj",
    )(x2, g, w_bf, gq2, gk2, cos_t, sin_t, sgu_w_bf, sgu_bias, sgu_g)


def _fourier_tables():
    n = np.arange(GRID_W)
    ang64 = 2.0 * np.pi * np.outer(n, n) / GRID_W
    c64, s64 = np.cos(ang64), np.sin(ang64)
    third = 1.0 / 8.0
    groups = FNET_WIDTH // HEAD_DIM
    eye = np.eye(groups)
    cs = np.concatenate([np.kron(eye, c64), -np.kron(eye, s64)], axis=1) * third
    f1 = np.block([[c64, s64], [-s64, c64]]) * third
    k = n[:, None] + GRID_W * n[None, :]
    ang = 2.0 * np.pi * k[:, :, None] * n[None, None, :] / (GRID_W * GRID_W)
    g2 = np.concatenate([np.cos(ang), np.sin(ang)], axis=2) * third
    return tuple(jnp.asarray(a, F32) for a in (cs, f1, g2))


FOURIER_UNROLL = 8


def _fourier_kernel(z_ref, cs_ref, f1_ref, g2_ref, y_ref, mid_ref):
    halves = range(FNET_WIDTH // LANES)

    def stage_a(grp, carry):
        b0 = grp * FOURIER_UNROLL
        xs = []
        for d in range(FOURIER_UNROLL):
            rows = pl.ds(b0 + d, GRID_W, stride=GRID_W)
            xs.append(jnp.concatenate([z_ref[j, 0, rows, :] for j in halves], axis=-1))
        xb = jnp.concatenate(xs, axis=0).astype(BF16)
        ri = jnp.dot(xb, cs_ref[...].astype(BF16), preferred_element_type=F32)
        st = jnp.concatenate(
            [jnp.concatenate([ri[d * GRID_W:(d + 1) * GRID_W, :FNET_WIDTH],
                              ri[d * GRID_W:(d + 1) * GRID_W, FNET_WIDTH:]], axis=0)
             for d in range(FOURIER_UNROLL)], axis=1).astype(BF16)
        t = jnp.dot(f1_ref[...].astype(BF16), st, preferred_element_type=F32)
        for d in range(FOURIER_UNROLL):
            dst = pl.ds(pl.multiple_of((b0 + d) * 2 * GRID_W, 2 * GRID_W), 2 * GRID_W)
            for j in halves:
                lanes = slice(d * FNET_WIDTH + j * LANES, d * FNET_WIDTH + (j + 1) * LANES)
                mid_ref[j, dst, :] = t[:, lanes]
        return carry

    lax.fori_loop(0, GRID_W // FOURIER_UNROLL, stage_a, 0)

    def stage_b(k1, carry):
        re_rows = pl.ds(k1, GRID_W, stride=2 * GRID_W)
        im_rows = pl.ds(k1 + GRID_W, GRID_W, stride=2 * GRID_W)
        tr = jnp.concatenate([mid_ref[j, re_rows, :] for j in halves], axis=-1)
        ti = jnp.concatenate([mid_ref[j, im_rows, :] for j in halves], axis=-1)
        st = jnp.concatenate([tr, ti], axis=0).astype(BF16)
        out = jnp.dot(g2_ref[k1].astype(BF16), st, preferred_element_type=F32)
        for j in halves:
            y_ref[j, 0, pl.ds(k1, GRID_W, stride=GRID_W), :] = out[:, j * LANES:(j + 1) * LANES]
        return carry

    lax.fori_loop(0, GRID_W, stage_b, 0, unroll=FOURIER_UNROLL)


def _fourier(zf4, tables):
    nh, b, s, w = zf4.shape
    cs, f1, g2 = tables
    return pl.pallas_call(
        _fourier_kernel,
        grid=(b,),
        in_specs=[
            pl.BlockSpec((nh, 1, s, w), lambda i: (0, i, 0, 0)),
            pl.BlockSpec(cs.shape, lambda i: (0, 0)),
            pl.BlockSpec(f1.shape, lambda i: (0, 0)),
            pl.BlockSpec(g2.shape, lambda i: (0, 0, 0)),
        ],
        out_specs=pl.BlockSpec((nh, 1, s, w), lambda i: (0, i, 0, 0)),
        out_shape=jax.ShapeDtypeStruct((nh, b, s, w), F32),
        scratch_shapes=[pltpu.VMEM((nh, 2 * s, w), F32)],
        compiler_params=_params("parallel"),
        name="fourier",
    )(zf4, cs, f1, g2)


ATTN_UNIT = 256
ATTN_CHUNK = 64
ATTN_PIECES = 16
ATTN_SCORE_DOTS = 8


def _attn_kernel(q_ref, k_ref, vt_ref, o_ref, qpad_ref, s0_ref, s1_ref, p0_ref, p1_ref,
                 m0_ref, m1_ref, a0_ref, a1_ref, ot_ref):
    s_refs, p_refs, m_refs = (s0_ref, s1_ref), (p0_ref, p1_ref), (m0_ref, m1_ref)
    a_refs = (a0_ref, a1_ref)
    tq = q_ref.shape[1]
    seq = k_ref.shape[1]
    units_per_head = tq // ATTN_UNIT
    n_units = N_Q_HEADS * units_per_head
    units_per_kv = n_units // N_KV_HEADS
    lane = lax.broadcasted_iota(jnp.int32, (1, LANES), 1)

    for hd in range(N_Q_HEADS):
        kv = hd // Q_PER_KV
        qs = q_ref[0, :, (hd // 2) * LANES:(hd // 2 + 1) * LANES].astype(F32)
        if hd % 2 != kv:
            qs = pltpu.roll(qs, HEAD_DIM, 1)
        qpad_ref[hd * tq:(hd + 1) * tq, :] = jnp.where((lane // HEAD_DIM) == kv, qs, 0.0).astype(BF16)

    def unit_rows(u):
        return slice(u * ATTN_UNIT, (u + 1) * ATTN_UNIT)

    piece = seq // ATTN_PIECES

    def stage_a(u, slot):
        q_u = qpad_ref[unit_rows(u), :]
        span = seq // ATTN_SCORE_DOTS
        pieces_per_dot = ATTN_PIECES // ATTN_SCORE_DOTS
        dots, best = [], []

        def step(i):
            if i % pieces_per_dot == 0:
                keys = slice((i // pieces_per_dot) * span, (i // pieces_per_dot + 1) * span)
                dots[:] = [lax.dot_general(k_ref[0, keys, :], q_u, (((1,), (1,)), ((), ())),
                                           preferred_element_type=F32)]
            local = (i % pieces_per_dot) * piece
            s = dots[0][local:local + piece]
            s_refs[slot][i * piece:(i + 1) * piece, :] = s
            parts = [s[r:r + SUBLANES] for r in range(0, piece, SUBLANES)]
            while len(parts) > 1:
                parts = [jnp.maximum(parts[j], parts[j + 1]) for j in range(0, len(parts), 2)]
            best[:] = [parts[0] if not best else jnp.maximum(best[0], parts[0])]

        def finish():
            m = best[0]
            m_refs[slot][...] = jnp.broadcast_to(jnp.max(m, axis=0, keepdims=True), m.shape)

        return [functools.partial(step, i) for i in range(ATTN_PIECES)], finish

    def stage_b(slot):
        s_ref, p_ref = s_refs[slot], p_refs[slot]

        def step(i):
            mb = jnp.broadcast_to(m_refs[slot][0:1, :], (ATTN_CHUNK, ATTN_UNIT))
            for c in range(i * piece // ATTN_CHUNK, (i + 1) * piece // ATTN_CHUNK):
                rows = slice(c * ATTN_CHUNK, (c + 1) * ATTN_CHUNK)
                p_ref[rows, :] = jnp.exp2(s_ref[rows, :] - mb).astype(BF16)

        return [functools.partial(step, i) for i in range(ATTN_PIECES)], None

    def stage_c(u, slot):
        a_ref = a_refs[slot]

        def step(i):
            keys = slice(i * piece, (i + 1) * piece)
            part = jnp.dot(vt_ref[0, u // units_per_kv, :, keys], p_refs[slot][keys, :],
                           preferred_element_type=F32)
            a_ref[...] = part if i == 0 else a_ref[...] + part

        def finish():
            ot_ref[u] = a_ref[:HEAD_DIM, :] / a_ref[HEAD_DIM:HEAD_DIM + 1, :]

        return [functools.partial(step, i) for i in range(ATTN_PIECES)], finish

    def issue(*stages):
        for steps in zip(*[st[0] for st in stages]):
            for step in steps:
                step()
        for st in stages:
            if st[1] is not None:
                st[1]()

    issue(stage_a(0, 0))
    issue(stage_a(1, 1), stage_b(0))
    for t in range(2, n_units):
        issue(stage_a(t, t % 2), stage_b((t - 1) % 2), stage_c(t - 2, t % 2))
    issue(stage_b((n_units - 1) % 2), stage_c(n_units - 2, n_units % 2))
    issue(stage_c(n_units - 1, (n_units - 1) % 2))

    for j in range(Q_WIDTH // LANES):
        for r in range(units_per_head):
            pair = jnp.concatenate([ot_ref[(2 * j) * units_per_head + r],
                                    ot_ref[(2 * j + 1) * units_per_head + r]], axis=0)
            o_ref[0, r * ATTN_UNIT:(r + 1) * ATTN_UNIT, j * LANES:(j + 1) * LANES] = pair.T


def _attention(q3, k3, vt4, tq=512):
    b, s, _ = q3.shape
    n_units = N_Q_HEADS * tq // ATTN_UNIT
    return pl.pallas_call(
        _attn_kernel,
        grid=(b, s // tq),
        in_specs=[
            pl.BlockSpec((1, tq, Q_WIDTH), lambda i, j: (i, j, 0)),
            pl.BlockSpec((1, s, KV_WIDTH), lambda i, j: (i, 0, 0)),
            pl.BlockSpec((1, N_KV_HEADS, VT_ROWS, s), lambda i, j: (i, 0, 0, 0)),
        ],
        out_specs=pl.BlockSpec((1, tq, Q_WIDTH), lambda i, j: (i, j, 0)),
        out_shape=jax.ShapeDtypeStruct((b, s, Q_WIDTH), F32),
        scratch_shapes=[
            pltpu.VMEM((N_Q_HEADS * tq, LANES), BF16),
            pltpu.VMEM((s, ATTN_UNIT), F32),
            pltpu.VMEM((s, ATTN_UNIT), F32),
            pltpu.VMEM((s, ATTN_UNIT), BF16),
            pltpu.VMEM((s, ATTN_UNIT), BF16),
            pltpu.VMEM((SUBLANES, ATTN_UNIT), F32),
            pltpu.VMEM((SUBLANES, ATTN_UNIT), F32),
            pltpu.VMEM((VT_ROWS, ATTN_UNIT), F32),
            pltpu.VMEM((VT_ROWS, ATTN_UNIT), F32),
            pltpu.VMEM((n_units, HEAD_DIM, ATTN_UNIT), F32),
        ],
        compiler_params=_params("parallel", "parallel"),
        name="attention",
    )(q3, k3, vt4)


FFN_CHUNK = 256


def _mix_ffn_kernel(yf_ref, ys_ref, ya_ref, x_ref, gm_ref, wo_ref, gpm_ref,
                    gpre_ref, wg_ref, wu_ref, wd_ref, gpost_ref, o_ref):
    a1 = FNET_WIDTH + SGU_WIDTH
    gm = gm_ref[...]
    x1 = []
    for rs in range(x_ref.shape[0] // ROW_SPLIT):
        rows = slice(rs * ROW_SPLIT, (rs + 1) * ROW_SPLIT)
        yf = jnp.concatenate([yf_ref[j, rows, :] for j in range(FNET_WIDTH // LANES)], axis=-1)
        y = jnp.concatenate([
            _rms(yf, gm[:, :FNET_WIDTH]),
            _rms(ys_ref[rows, :], gm[:, FNET_WIDTH:a1]),
            _rms(ya_ref[rows, :], gm[:, a1:]),
        ], axis=-1).astype(BF16)
        m = jnp.dot(y, wo_ref[...], preferred_element_type=F32)
        x1.append(x_ref[rows, :] + _rms(m, gpm_ref[...]))
    x1 = jnp.concatenate(x1, axis=0)
    h = _rms(x1, gpre_ref[...]).astype(BF16)
    acc = None
    for c in range(wg_ref.shape[1] // FFN_CHUNK):
        cols = slice(c * FFN_CHUNK, (c + 1) * FFN_CHUNK)
        gate = jnp.dot(h, wg_ref[:, cols], preferred_element_type=F32)
        up = jnp.dot(h, wu_ref[:, cols], preferred_element_type=F32)
        act = (gate * (1.0 / (1.0 + jnp.exp(-gate))) * up).astype(BF16)
        part = jnp.dot(act, wd_ref[cols, :], preferred_element_type=F32)
        acc = part if acc is None else acc + part
    o_ref[...] = x1 + _rms(acc, gpost_ref[...])


def _mix_ffn(yf2, ys2, ya2, x2, gm, wo_bf, gpm, gpre, wg_bf, wu_bf, wd_bf, gpost, tm=512):
    t = x2.shape[0]
    d_ff = wg_bf.shape[1]
    row = lambda i: (i, 0)
    fixed = lambda i: (0, 0)
    resident = pl.Buffered(1)
    return pl.pallas_call(
        _mix_ffn_kernel,
        grid=(t // tm,),
        in_specs=[
            pl.BlockSpec((FNET_WIDTH // LANES, tm, LANES), lambda i: (0, i, 0)),
            pl.BlockSpec((tm, SGU_WIDTH), row),
            pl.BlockSpec((tm, Q_WIDTH), row),
            pl.BlockSpec((tm, D_MODEL), row),
            pl.BlockSpec((1, D_MIX), fixed),
            pl.BlockSpec((D_MIX, D_MODEL), fixed, pipeline_mode=resident),
            pl.BlockSpec((1, D_MODEL), fixed),
            pl.BlockSpec((1, D_MODEL), fixed),
            pl.BlockSpec((D_MODEL, d_ff), fixed, pipeline_mode=resident),
            pl.BlockSpec((D_MODEL, d_ff), fixed, pipeline_mode=resident),
            pl.BlockSpec((d_ff, D_MODEL), fixed, pipeline_mode=resident),
            pl.BlockSpec((1, D_MODEL), fixed),
        ],
        out_specs=pl.BlockSpec((tm, D_MODEL), row),
        out_shape=jax.ShapeDtypeStruct((t, D_MODEL), F32),
        compiler_params=_params("parallel"),
        name="mix_ffn",
    )(yf2, ys2, ya2, x2, gm, wo_bf, gpm, gpre, wg_bf, wu_bf, wd_bf, gpost)


def _rope_tables(seq):
    pos = jnp.arange(seq)
    row = (pos // GRID_W).astype(F32)
    col = (pos % GRID_W).astype(F32)
    freqs = ROPE_THETA ** (-jnp.arange(ROPE_FREQS, dtype=F32) / ROPE_FREQS)
    ang_r = row[:, None] * freqs
    ang_c = col[:, None] * freqs
    cos_r, sin_r, cos_c, sin_c = jnp.cos(ang_r), jnp.sin(ang_r), jnp.cos(ang_c), jnp.sin(ang_c)
    cos_h = jnp.concatenate([cos_r, cos_r, cos_c, cos_c], axis=-1)
    sin_h = jnp.concatenate([-sin_r, sin_r, -sin_c, sin_c], axis=-1)
    reps = LANES // HEAD_DIM
    return jnp.tile(cos_h, (1, reps)), jnp.tile(sin_h, (1, reps))


@jax.jit
def kernel(x, g_pre_mix, w_in, sgu_w, sgu_b, sgu_g, g_q, g_k, g_mix, w_out,
           g_post_mix, g_pre_ffn, w_gate, w_up, w_down, g_post_ffn):
    b, s, d = x.shape
    depth = w_in.shape[0]
    t = b * s
    cos_t, sin_t = _rope_tables(s)
    ftabs = _fourier_tables()
    reps = LANES // HEAD_DIM
    x2 = x.reshape(t, d)
    for l in range(depth):
        bias_full = jnp.repeat(sgu_b[l].T, HEAD_DIM, axis=1)
        zf, ys, q, k, vt = _inproj(
            x2, g_pre_mix[l][None], w_in[l].astype(BF16),
            jnp.tile(g_q[l], reps)[None], jnp.tile(g_k[l], reps)[None], cos_t, sin_t,
            sgu_w[l].astype(BF16), bias_full, sgu_g[l][None], s)
        nh = FNET_WIDTH // LANES
        yf = _fourier(zf.reshape(nh, b, s, LANES), ftabs).reshape(nh, t, LANES)
        ya = _attention(q.reshape(b, s, Q_WIDTH), k.reshape(b, s, KV_WIDTH),
                        vt).reshape(t, Q_WIDTH)
        x2 = _mix_ffn(yf, ys, ya, x2, g_mix[l][None], w_out[l].astype(BF16), g_post_mix[l][None],
                      g_pre_ffn[l][None], w_gate[l].astype(BF16), w_up[l].astype(BF16),
                      w_down[l].astype(BF16), g_post_ffn[l][None])
    return x2.reshape(b, s, d)
```

```python
import functools
import math

import numpy as np
import jax
import jax.numpy as jnp
from jax import lax
from jax.experimental import pallas as pl
from jax.experimental.pallas import tpu as pltpu

F32 = jnp.float32
BF16 = jnp.bfloat16

D_MODEL = 1024
GRID_W = 64
HEAD_DIM = 64
FNET_WIDTH = 256
SGU_WIDTH = 256
SGU_HEADS = 4
SGU_CHUNK = 128
N_Q_HEADS = 8
N_KV_HEADS = 2
Q_PER_KV = N_Q_HEADS // N_KV_HEADS
Q_WIDTH = N_Q_HEADS * HEAD_DIM
KV_WIDTH = N_KV_HEADS * HEAD_DIM
ROPE_THETA = 10000.0
ROPE_FREQS = HEAD_DIM // 4
D_MIX = FNET_WIDTH + SGU_WIDTH + Q_WIDTH
D_IN = FNET_WIDTH + 2 * SGU_WIDTH + Q_WIDTH + 2 * KV_WIDTH
EPS = 1e-6

LANES = 128
SUBLANES = 8
BF16_SUBLANES = 16
VT_ROWS = HEAD_DIM + BF16_SUBLANES
VMEM_LIMIT = 56 * 1024 * 1024

O_SGU = FNET_WIDTH
O_Q = O_SGU + 2 * SGU_WIDTH
O_K = O_Q + Q_WIDTH
O_V = O_K + KV_WIDTH


def _params(*sem):
    return pltpu.CompilerParams(dimension_semantics=sem, vmem_limit_bytes=VMEM_LIMIT)


ROW_SPLIT = 256


def _rms(x, g):
    return x * lax.rsqrt(jnp.mean(x * x, axis=-1, keepdims=True) + EPS) * g


def _head_norm_rope(xs, g, cos, sin, seg_ones, swap_lo):
    ms = jnp.dot((xs * xs).astype(BF16), seg_ones, preferred_element_type=F32) * (1.0 / HEAD_DIM)
    xn = xs * lax.rsqrt(ms + EPS) * g
    sw = jnp.where(swap_lo, pltpu.roll(xn, LANES - ROPE_FREQS, 1), pltpu.roll(xn, ROPE_FREQS, 1))
    return xn * cos + sw * sin


def _gelu_tanh(x):
    c = math.sqrt(2.0 / math.pi)
    return 0.5 * x * (1.0 + jnp.tanh(c * (x + 0.044715 * (x * x * x))))


def _spatial_gate(zs, w_ref, bias, g):
    z = _gelu_tanh(zs)
    u = z[:, :SGU_WIDTH]
    v = z[:, SGU_WIDTH:]
    d = v - jnp.mean(v, axis=-1, keepdims=True)
    var = jnp.mean(d * d, axis=-1, keepdims=True)
    vn = (d * lax.rsqrt(var + EPS) * g).astype(BF16)
    head = lax.broadcasted_iota(jnp.int32, (1, SGU_WIDTH), 1) // HEAD_DIM
    out = []
    for c in range(zs.shape[0] // SGU_CHUNK):
        rows = slice(c * SGU_CHUNK, (c + 1) * SGU_CHUNK)
        vc = vn[rows]
        sv = jnp.dot(w_ref[0], vc, preferred_element_type=F32)
        for hd in range(1, SGU_HEADS):
            sv = jnp.where(head == hd, jnp.dot(w_ref[hd], vc, preferred_element_type=F32), sv)
        out.append(u[rows] * (sv + bias))
    return jnp.concatenate(out, axis=0)


def _inproj_kernel(x_ref, g_ref, w_ref, gq_ref, gk_ref, cos_ref, sin_ref, sw_ref, sb_ref, sg_ref,
                   zf_ref, ys_ref, q_ref, k_ref, vt_ref):
    r = lax.broadcasted_iota(jnp.int32, (LANES, LANES), 0) // HEAD_DIM
    c = lax.broadcasted_iota(jnp.int32, (LANES, LANES), 1) // HEAD_DIM
    seg_ones = jnp.where(r == c, 1.0, 0.0).astype(BF16)
    lane = lax.broadcasted_iota(jnp.int32, (1, LANES), 1)
    swap_lo = (lane % (2 * ROPE_FREQS)) < ROPE_FREQS
    scale = HEAD_DIM ** -0.5 * math.log2(math.e)
    ones = jnp.ones((VT_ROWS - HEAD_DIM, ROW_SPLIT), BF16)
    n_groups = x_ref.shape[0] // ROW_SPLIT

    def project(rs):
        rows = slice(rs * ROW_SPLIT, (rs + 1) * ROW_SPLIT)
        h = _rms(x_ref[rows, :], g_ref[...]).astype(BF16)
        return jnp.dot(h, w_ref[...], preferred_element_type=F32)

    z_next = project(0)
    for rs in range(n_groups):
        rows = slice(rs * ROW_SPLIT, (rs + 1) * ROW_SPLIT)
        z = z_next
        if rs + 1 < n_groups:
            z_next = project(rs + 1)
        for j in range(FNET_WIDTH // LANES):
            zf_ref[j, rows, :] = z[:, j * LANES:(j + 1) * LANES]
        ys_ref[rows, :] = _spatial_gate(z[:, O_SGU:O_Q], sw_ref, sb_ref[...], sg_ref[...])
        cos = cos_ref[rows, :]
        sin = sin_ref[rows, :]
        for j in range(Q_WIDTH // LANES):
            qs = z[:, O_Q + j * LANES:O_Q + (j + 1) * LANES]
            qr = _head_norm_rope(qs, gq_ref[...], cos, sin, seg_ones, swap_lo)
            q_ref[rows, j * LANES:(j + 1) * LANES] = (qr * scale).astype(BF16)
        k_ref[rows, :] = _head_norm_rope(z[:, O_K:O_V], gk_ref[...], cos, sin, seg_ones,
                                         swap_lo).astype(BF16)
        vt = z[:, O_V:].T
        for kv in range(N_KV_HEADS):
            vt_ref[0, kv, :HEAD_DIM, rows] = vt[kv * HEAD_DIM:(kv + 1) * HEAD_DIM].astype(BF16)
            vt_ref[0, kv, HEAD_DIM:, rows] = ones


def _inproj(x2, g, w_bf, layer, gq2, gk2, cos_t, sin_t, sgu_w_bf, sgu_bias, sgu_g, seq, tm=1024):
    t = x2.shape[0]
    n_seq = seq // tm
    row = lambda i: (i, 0)
    fixed = lambda i: (0, 0)
    tab = lambda i: (i % n_seq, 0)
    return pl.pallas_call(
        _inproj_kernel,
        grid=(t // tm,),
        in_specs=[
            pl.BlockSpec((tm, D_MODEL), row),
            pl.BlockSpec((1, D_MODEL), fixed),
            pl.BlockSpec((None, D_MODEL, D_IN), lambda i: (layer, 0, 0)),
            pl.BlockSpec((1, LANES), fixed),
            pl.BlockSpec((1, LANES), fixed),
            pl.BlockSpec((tm, LANES), tab),
            pl.BlockSpec((tm, LANES), tab),
            pl.BlockSpec(sgu_w_bf.shape, lambda i: (0, 0, 0)),
            pl.BlockSpec(sgu_bias.shape, fixed),
            pl.BlockSpec((1, SGU_WIDTH), fixed),
        ],
        out_specs=[
            pl.BlockSpec((FNET_WIDTH // LANES, tm, LANES), lambda i: (0, i, 0)),
            pl.BlockSpec((tm, SGU_WIDTH), row),
            pl.BlockSpec((tm, Q_WIDTH), row),
            pl.BlockSpec((tm, KV_WIDTH), row),
            pl.BlockSpec((1, N_KV_HEADS, VT_ROWS, tm), lambda i: (i // n_seq, 0, 0, i % n_seq)),
        ],
        out_shape=[
            jax.ShapeDtypeStruct((FNET_WIDTH // LANES, t, LANES), F32),
            jax.ShapeDtypeStruct((t, SGU_WIDTH), F32),
            jax.ShapeDtypeStruct((t, Q_WIDTH), BF16),
            jax.ShapeDtypeStruct((t, KV_WIDTH), BF16),
            jax.ShapeDtypeStruct((t // seq, N_KV_HEADS, VT_ROWS, seq), BF16),
        ],
        compiler_params=_params("parallel"),
        name="inproj",
    )(x2, g, w_bf, gq2, gk2, cos_t, sin_t, sgu_w_bf, sgu_bias, sgu_g)


def _fourier_tables():
    n = np.arange(GRID_W)
    ang64 = 2.0 * np.pi * np.outer(n, n) / GRID_W
    c64, s64 = np.cos(ang64), np.sin(ang64)
    third = 1.0 / 8.0
    groups = FNET_WIDTH // HEAD_DIM
    eye = np.eye(groups)
    cs = np.concatenate([np.kron(eye, c64), -np.kron(eye, s64)], axis=1) * third
    f1 = np.block([[c64, s64], [-s64, c64]]) * third
    k = n[:, None] + GRID_W * n[None, :]
    ang = 2.0 * np.pi * k[:, :, None] * n[None, None, :] / (GRID_W * GRID_W)
    g2 = np.concatenate([np.cos(ang), np.sin(ang)], axis=2) * third
    return tuple(jnp.asarray(a, F32) for a in (cs, f1, g2))


FOURIER_UNROLL = 8


def _fourier_kernel(z_ref, cs_ref, f1_ref, g2_ref, y_ref, mid_ref):
    halves = range(FNET_WIDTH // LANES)

    def stage_a(grp, carry):
        b0 = grp * FOURIER_UNROLL
        xs = []
        for d in range(FOURIER_UNROLL):
            rows = pl.ds(b0 + d, GRID_W, stride=GRID_W)
            xs.append(jnp.concatenate([z_ref[j, 0, rows, :] for j in halves], axis=-1))
        xb = jnp.concatenate(xs, axis=0).astype(BF16)
        ri = jnp.dot(xb, cs_ref[...].astype(BF16), preferred_element_type=F32)
        st = jnp.concatenate(
            [jnp.concatenate([ri[d * GRID_W:(d + 1) * GRID_W, :FNET_WIDTH],
                              ri[d * GRID_W:(d + 1) * GRID_W, FNET_WIDTH:]], axis=0)
             for d in range(FOURIER_UNROLL)], axis=1).astype(BF16)
        t = jnp.dot(f1_ref[...].astype(BF16), st, preferred_element_type=F32)
        for d in range(FOURIER_UNROLL):
            dst = pl.ds(pl.multiple_of((b0 + d) * 2 * GRID_W, 2 * GRID_W), 2 * GRID_W)
            for j in halves:
                lanes = slice(d * FNET_WIDTH + j * LANES, d * FNET_WIDTH + (j + 1) * LANES)
                mid_ref[j, dst, :] = t[:, lanes]
        return carry

    lax.fori_loop(0, GRID_W // FOURIER_UNROLL, stage_a, 0)

    def stage_b(k1, carry):
        re_rows = pl.ds(k1, GRID_W, stride=2 * GRID_W)
        im_rows = pl.ds(k1 + GRID_W, GRID_W, stride=2 * GRID_W)
        tr = jnp.concatenate([mid_ref[j, re_rows, :] for j in halves], axis=-1)
        ti = jnp.concatenate([mid_ref[j, im_rows, :] for j in halves], axis=-1)
        st = jnp.concatenate([tr, ti], axis=0).astype(BF16)
        out = jnp.dot(g2_ref[k1].astype(BF16), st, preferred_element_type=F32)
        for j in halves:
            y_ref[j, 0, pl.ds(k1, GRID_W, stride=GRID_W), :] = out[:, j * LANES:(j + 1) * LANES]
        return carry

    lax.fori_loop(0, GRID_W, stage_b, 0, unroll=FOURIER_UNROLL)


def _fourier(zf4, tables):
    nh, b, s, w = zf4.shape
    cs, f1, g2 = tables
    return pl.pallas_call(
        _fourier_kernel,
        grid=(b,),
        in_specs=[
            pl.BlockSpec((nh, 1, s, w), lambda i: (0, i, 0, 0)),
            pl.BlockSpec(cs.shape, lambda i: (0, 0)),
            pl.BlockSpec(f1.shape, lambda i: (0, 0)),
            pl.BlockSpec(g2.shape, lambda i: (0, 0, 0)),
        ],
        out_specs=pl.BlockSpec((nh, 1, s, w), lambda i: (0, i, 0, 0)),
        out_shape=jax.ShapeDtypeStruct((nh, b, s, w), F32),
        scratch_shapes=[pltpu.VMEM((nh, 2 * s, w), F32)],
        compiler_params=_params("parallel"),
        name="fourier",
    )(zf4, cs, f1, g2)


ATTN_UNIT = 256
ATTN_CHUNK = 64
ATTN_PIECES = 16


def _attn_kernel(q_ref, k_ref, vt_ref, o_ref, qpad_ref, s0_ref, s1_ref, p0_ref, p1_ref,
                 m0_ref, m1_ref, ot_ref):
    s_refs, p_refs, m_refs = (s0_ref, s1_ref), (p0_ref, p1_ref), (m0_ref, m1_ref)
    tq = q_ref.shape[1]
    seq = k_ref.shape[1]
    units_per_head = tq // ATTN_UNIT
    n_units = N_Q_HEADS * units_per_head
    units_per_kv = n_units // N_KV_HEADS
    lane = lax.broadcasted_iota(jnp.int32, (1, LANES), 1)

    for hd in range(N_Q_HEADS):
        kv = hd // Q_PER_KV
        qs = q_ref[0, :, (hd // 2) * LANES:(hd // 2 + 1) * LANES].astype(F32)
        if hd % 2 != kv:
            qs = pltpu.roll(qs, HEAD_DIM, 1)
        qpad_ref[hd * tq:(hd + 1) * tq, :] = jnp.where((lane // HEAD_DIM) == kv, qs, 0.0).astype(BF16)

    def unit_rows(u):
        return slice(u * ATTN_UNIT, (u + 1) * ATTN_UNIT)

    piece = seq // ATTN_PIECES

    def stage_a(u, slot):
        q_u = qpad_ref[unit_rows(u), :]
        acc = []

        def step(i):
            rows = slice(i * piece, (i + 1) * piece)
            s = lax.dot_general(k_ref[0, rows, :], q_u, (((1,), (1,)), ((), ())),
                                preferred_element_type=F32)
            s_refs[slot][rows, :] = s
            parts = [s[r:r + SUBLANES] for r in range(0, piece, SUBLANES)]
            while len(parts) > 1:
                parts = [jnp.maximum(parts[j], parts[j + 1]) for j in range(0, len(parts), 2)]
            acc.append(parts[0])

        def finish():
            m = functools.reduce(jnp.maximum, acc)
            m_refs[slot][...] = jnp.broadcast_to(jnp.max(m, axis=0, keepdims=True), m.shape)

        return [functools.partial(step, i) for i in range(ATTN_PIECES)], finish

    def stage_b(slot):
        s_ref, p_ref = s_refs[slot], p_refs[slot]

        def step(i):
            mb = jnp.broadcast_to(m_refs[slot][0:1, :], (ATTN_CHUNK, ATTN_UNIT))
            for c in range(i * piece // ATTN_CHUNK, (i + 1) * piece // ATTN_CHUNK):
                rows = slice(c * ATTN_CHUNK, (c + 1) * ATTN_CHUNK)
                p_ref[rows, :] = jnp.exp2(s_ref[rows, :] - mb).astype(BF16)

        return [functools.partial(step, i) for i in range(ATTN_PIECES)], None

    def stage_c(u, slot):
        acc = []

        def step(i):
            keys = slice(i * piece, (i + 1) * piece)
            acc.append(jnp.dot(vt_ref[0, u // units_per_kv, :, keys], p_refs[slot][keys, :],
                               preferred_element_type=F32))

        def finish():
            o = functools.reduce(jnp.add, acc)
            ot_ref[u] = o[:HEAD_DIM] / o[HEAD_DIM:HEAD_DIM + 1]

        return [functools.partial(step, i) for i in range(ATTN_PIECES)], finish

    def issue(*stages):
        for steps in zip(*[st[0] for st in stages]):
            for step in steps:
                step()
        for st in stages:
            if st[1] is not None:
                st[1]()

    issue(stage_a(0, 0))
    issue(stage_a(1, 1), stage_b(0))
    for t in range(2, n_units):
        issue(stage_a(t, t % 2), stage_b((t - 1) % 2), stage_c(t - 2, t % 2))
    issue(stage_b((n_units - 1) % 2), stage_c(n_units - 2, n_units % 2))
    issue(stage_c(n_units - 1, (n_units - 1) % 2))

    for j in range(Q_WIDTH // LANES):
        for r in range(units_per_head):
            pair = jnp.concatenate([ot_ref[(2 * j) * units_per_head + r],
                                    ot_ref[(2 * j + 1) * units_per_head + r]], axis=0)
            o_ref[0, r * ATTN_UNIT:(r + 1) * ATTN_UNIT, j * LANES:(j + 1) * LANES] = pair.T


def _attention(q3, k3, vt4, tq=512):
    b, s, _ = q3.shape
    n_units = N_Q_HEADS * tq // ATTN_UNIT
    return pl.pallas_call(
        _attn_kernel,
        grid=(b, s // tq),
        in_specs=[
            pl.BlockSpec((1, tq, Q_WIDTH), lambda i, j: (i, j, 0)),
            pl.BlockSpec((1, s, KV_WIDTH), lambda i, j: (i, 0, 0)),
            pl.BlockSpec((1, N_KV_HEADS, VT_ROWS, s), lambda i, j: (i, 0, 0, 0)),
        ],
        out_specs=pl.BlockSpec((1, tq, Q_WIDTH), lambda i, j: (i, j, 0)),
        out_shape=jax.ShapeDtypeStruct((b, s, Q_WIDTH), F32),
        scratch_shapes=[
            pltpu.VMEM((N_Q_HEADS * tq, LANES), BF16),
            pltpu.VMEM((s, ATTN_UNIT), F32),
            pltpu.VMEM((s, ATTN_UNIT), F32),
            pltpu.VMEM((s, ATTN_UNIT), BF16),
            pltpu.VMEM((s, ATTN_UNIT), BF16),
            pltpu.VMEM((SUBLANES, ATTN_UNIT), F32),
            pltpu.VMEM((SUBLANES, ATTN_UNIT), F32),
            pltpu.VMEM((n_units, HEAD_DIM, ATTN_UNIT), F32),
        ],
        compiler_params=_params("parallel", "parallel"),
        name="attention",
    )(q3, k3, vt4)


FFN_CHUNK = 256


def _mix_ffn_kernel(yf_ref, ys_ref, ya_ref, x_ref, gm_ref, wo_ref, gpm_ref,
                    gpre_ref, wg_ref, wu_ref, wd_ref, gpost_ref, o_ref):
    a1 = FNET_WIDTH + SGU_WIDTH
    gm = gm_ref[...]
    x1 = []
    for rs in range(x_ref.shape[0] // ROW_SPLIT):
        rows = slice(rs * ROW_SPLIT, (rs + 1) * ROW_SPLIT)
        yf = jnp.concatenate([yf_ref[j, rows, :] for j in range(FNET_WIDTH // LANES)], axis=-1)
        y = jnp.concatenate([
            _rms(yf, gm[:, :FNET_WIDTH]),
            _rms(ys_ref[rows, :], gm[:, FNET_WIDTH:a1]),
            _rms(ya_ref[rows, :], gm[:, a1:]),
        ], axis=-1).astype(BF16)
        m = jnp.dot(y, wo_ref[...], preferred_element_type=F32)
        x1.append(x_ref[rows, :] + _rms(m, gpm_ref[...]))
    x1 = jnp.concatenate(x1, axis=0)
    h = _rms(x1, gpre_ref[...]).astype(BF16)
    acc = None
    for c in range(wg_ref.shape[1] // FFN_CHUNK):
        cols = slice(c * FFN_CHUNK, (c + 1) * FFN_CHUNK)
        gate = jnp.dot(h, wg_ref[:, cols], preferred_element_type=F32)
        up = jnp.dot(h, wu_ref[:, cols], preferred_element_type=F32)
        act = (gate * (1.0 / (1.0 + jnp.exp(-gate))) * up).astype(BF16)
        part = jnp.dot(act, wd_ref[cols, :], preferred_element_type=F32)
        acc = part if acc is None else acc + part
    o_ref[...] = x1 + _rms(acc, gpost_ref[...])


def _mix_ffn(yf2, ys2, ya2, x2, gm, wo_bf, gpm, gpre, wg_bf, wu_bf, wd_bf, gpost, layer, tm=512):
    t = x2.shape[0]
    d_ff = wg_bf.shape[2]
    row = lambda i: (i, 0)
    fixed = lambda i: (0, 0)
    this_layer = lambda i: (layer, 0, 0)
    resident = pl.Buffered(1)
    return pl.pallas_call(
        _mix_ffn_kernel,
        grid=(t // tm,),
        in_specs=[
            pl.BlockSpec((FNET_WIDTH // LANES, tm, LANES), lambda i: (0, i, 0)),
            pl.BlockSpec((tm, SGU_WIDTH), row),
            pl.BlockSpec((tm, Q_WIDTH), row),
            pl.BlockSpec((tm, D_MODEL), row),
            pl.BlockSpec((1, D_MIX), fixed),
            pl.BlockSpec((None, D_MIX, D_MODEL), this_layer, pipeline_mode=resident),
            pl.BlockSpec((1, D_MODEL), fixed),
            pl.BlockSpec((1, D_MODEL), fixed),
            pl.BlockSpec((None, D_MODEL, d_ff), this_layer, pipeline_mode=resident),
            pl.BlockSpec((None, D_MODEL, d_ff), this_layer, pipeline_mode=resident),
            pl.BlockSpec((None, d_ff, D_MODEL), this_layer, pipeline_mode=resident),
            pl.BlockSpec((1, D_MODEL), fixed),
        ],
        out_specs=pl.BlockSpec((tm, D_MODEL), row),
        out_shape=jax.ShapeDtypeStruct((t, D_MODEL), F32),
        compiler_params=_params("parallel"),
        name="mix_ffn",
    )(yf2, ys2, ya2, x2, gm, wo_bf, gpm, gpre, wg_bf, wu_bf, wd_bf, gpost)


CAST_ROW_BLOCKS = 4


def _cast_kernel(w_ref, o_ref):
    o_ref[...] = w_ref[...].astype(BF16)


def _to_bf16(w):
    depth, rows, cols = w.shape
    tr = rows // CAST_ROW_BLOCKS
    spec = pl.BlockSpec((1, tr, cols), lambda l, i: (l, i, 0))
    return pl.pallas_call(
        _cast_kernel,
        grid=(depth, CAST_ROW_BLOCKS),
        in_specs=[spec],
        out_specs=spec,
        out_shape=jax.ShapeDtypeStruct(w.shape, BF16),
        compiler_params=_params("parallel", "parallel"),
        name="cast_bf16",
    )(w)


def _rope_tables(seq):
    pos = jnp.arange(seq)
    row = (pos // GRID_W).astype(F32)
    col = (pos % GRID_W).astype(F32)
    freqs = ROPE_THETA ** (-jnp.arange(ROPE_FREQS, dtype=F32) / ROPE_FREQS)
    ang_r = row[:, None] * freqs
    ang_c = col[:, None] * freqs
    cos_r, sin_r, cos_c, sin_c = jnp.cos(ang_r), jnp.sin(ang_r), jnp.cos(ang_c), jnp.sin(ang_c)
    cos_h = jnp.concatenate([cos_r, cos_r, cos_c, cos_c], axis=-1)
    sin_h = jnp.concatenate([-sin_r, sin_r, -sin_c, sin_c], axis=-1)
    reps = LANES // HEAD_DIM
    return jnp.tile(cos_h, (1, reps)), jnp.tile(sin_h, (1, reps))


@jax.jit
def kernel(x, g_pre_mix, w_in, sgu_w, sgu_b, sgu_g, g_q, g_k, g_mix, w_out,
           g_post_mix, g_pre_ffn, w_gate, w_up, w_down, g_post_ffn):
    b, s, d = x.shape
    depth = w_in.shape[0]
    t = b * s
    cos_t, sin_t = _rope_tables(s)
    ftabs = _fourier_tables()
    reps = LANES // HEAD_DIM
    w_in_bf, w_out_bf = _to_bf16(w_in), _to_bf16(w_out)
    w_gate_bf, w_up_bf, w_down_bf = _to_bf16(w_gate), _to_bf16(w_up), _to_bf16(w_down)
    x2 = x.reshape(t, d)
    for l in range(depth):
        bias_full = jnp.repeat(sgu_b[l].T, HEAD_DIM, axis=1)
        zf, ys, q, k, vt = _inproj(
            x2, g_pre_mix[l][None], w_in_bf, l,
            jnp.tile(g_q[l], reps)[None], jnp.tile(g_k[l], reps)[None], cos_t, sin_t,
            sgu_w[l].astype(BF16), bias_full, sgu_g[l][None], s)
        nh = FNET_WIDTH // LANES
        yf = _fourier(zf.reshape(nh, b, s, LANES), ftabs).reshape(nh, t, LANES)
        ya = _attention(q.reshape(b, s, Q_WIDTH), k.reshape(b, s, KV_WIDTH),
                        vt).reshape(t, Q_WIDTH)
        x2 = _mix_ffn(yf, ys, ya, x2, g_mix[l][None], w_out_bf, g_post_mix[l][None],
                      g_pre_ffn[l][None], w_gate_bf, w_up_bf, w_down_bf, g_post_ffn[l][None], l)
    return x2.reshape(b, s, d)
```
